```python
import jax, jax.numpy as jnp
from jax import lax
import numpy as np

D_MODEL = 1024
BATCH = 8
SEQ = 2048
DEPTH = 2
DEC_BATCH = 128
DEC_SEQ = 1
PAST_LEN = 16384
PAGE_SIZE = 128

RET_HEADS = 4
RET_DK = 128
RET_DV = 128
RET_WIDTH = RET_HEADS * RET_DV
GDN_HEADS = 4
GDN_DK = 128
GDN_DV = 128
GDN_WIDTH = GDN_HEADS * GDN_DV
CONV_W = 4
CONV_CH = GDN_HEADS * (2 * GDN_DK + GDN_DV)
IN_SPLITS = [RET_WIDTH, 2 * RET_WIDTH, 3 * RET_WIDTH, 4 * RET_WIDTH,
             4 * RET_WIDTH + CONV_CH,
             4 * RET_WIDTH + CONV_CH + GDN_WIDTH,
             4 * RET_WIDTH + CONV_CH + GDN_WIDTH + GDN_HEADS]
IN_WIDTH = 4 * RET_WIDTH + CONV_CH + GDN_WIDTH + 2 * GDN_HEADS
RET_CHUNK = 128
GDN_CHUNK = 64
ROPE_BASE = 10000.0
N_EXPERTS = 32
TOP_K = 4
D_EXPERT = D_MODEL
SWIGLU_LIMIT = 7.0
SWIGLU_ALPHA = 1.702
MOE_BLOCK = 256
LN_EPS = 1e-5
NORM_EPS = 1e-6
DEEPNORM_ALPHA = (2.0 * DEPTH) ** 0.25
DEEPNORM_BETA = (8.0 * DEPTH) ** -0.25

kernel_name = 'hybrid_retention_gdn_moe_deepnorm_step'


def _layer_norm(x, g, b):
    xf = x.astype(jnp.float32)
    mu = jnp.mean(xf, axis=-1, keepdims=True)
    var = jnp.mean(jnp.square(xf - mu), axis=-1, keepdims=True)
    y = (xf - mu) * lax.rsqrt(var + LN_EPS) * g.astype(jnp.float32) + b.astype(jnp.float32)
    return y.astype(x.dtype)


def _l2norm(x):
    return x * lax.rsqrt(jnp.sum(x * x, axis=-1, keepdims=True) + 1e-6)


def _chunk_len(t, c):
    return c if t % c == 0 else t


def _to_chunks(x, c):
    b, t, h = x.shape[:3]
    x = x.reshape((b, t // c, c, h) + x.shape[3:])
    return jnp.transpose(x, (1, 0, 3, 2) + tuple(range(4, x.ndim)))


def _from_chunks(o):
    n, b, h, c, d = o.shape
    return jnp.transpose(o, (1, 0, 3, 2, 4)).reshape(b, n * c, h, d)


def _rope(x, offset):
    t, half = x.shape[1], x.shape[-1] // 2
    inv_freq = ROPE_BASE ** (-jnp.arange(half, dtype=jnp.float32) / half)
    pos = jnp.arange(t, dtype=jnp.float32) + offset
    ang = pos[:, None] * inv_freq[None, :]
    cos = jnp.cos(ang)[None, :, None, :]
    sin = jnp.sin(ang)[None, :, None, :]
    x1, x2 = x[..., :half], x[..., half:]
    return jnp.concatenate([x1 * cos - x2 * sin, x2 * cos + x1 * sin], axis=-1)


def _retention(q, k, v, s0):
    t = q.shape[1]
    c = _chunk_len(t, RET_CHUNK)
    log_gamma = jnp.log(1.0 - 2.0 ** (-5.0 - jnp.arange(RET_HEADS, dtype=jnp.float32)))
    idx = jnp.arange(c, dtype=jnp.float32)
    diff = idx[:, None] - idx[None, :]
    causal = diff >= 0
    intra = jnp.where(causal, jnp.exp(log_gamma[:, None, None] * jnp.where(causal, diff, 0.0)), 0.0)
    q_dec = jnp.exp(log_gamma[:, None] * (idx + 1.0))[..., None]
    k_dec = jnp.exp(log_gamma[:, None] * (c - 1.0 - idx))[..., None]
    c_dec = jnp.exp(log_gamma * c)[:, None, None]

    def step(s, inp):
        qc, kc, vc = inp
        att = jnp.einsum('bhid,bhjd->bhij', qc, kc) * intra
        o = jnp.einsum('bhij,bhjv->bhiv', att, vc) + jnp.einsum('bhid,bhdv->bhiv', qc * q_dec, s)
        s = s * c_dec + jnp.einsum('bhjd,bhjv->bhdv', kc * k_dec, vc)
        return s, o

    s, o = lax.scan(step, s0, (_to_chunks(q, c), _to_chunks(k, c), _to_chunks(v, c)))
    return _from_chunks(o), s


def _gated_delta(q, k, v, g, beta, s0):
    t = q.shape[1]
    c = _chunk_len(t, GDN_CHUNK)
    idx = jnp.arange(c)
    causal = idx[:, None] >= idx[None, :]
    strict = idx[:, None] > idx[None, :]
    eye = jnp.eye(c, dtype=jnp.float32)

    def step(s, inp):
        qc, kc, vc, gc, bc = inp
        gcum = jnp.cumsum(gc, axis=-1)
        diff = gcum[..., :, None] - gcum[..., None, :]
        decay = jnp.where(causal, jnp.exp(jnp.where(causal, diff, 0.0)), 0.0)
        kk = jnp.einsum('bhik,bhjk->bhij', kc, kc)
        a_mat = eye + jnp.where(strict, bc[..., :, None] * kk * decay, 0.0)
        rhs = jnp.concatenate([vc * bc[..., None], kc * (bc * jnp.exp(gcum))[..., None]], axis=-1)
        sol = lax.linalg.triangular_solve(a_mat, rhs, left_side=True, lower=True, unit_diagonal=True)
        u, w = sol[..., :GDN_DV], sol[..., GDN_DV:]
        v_new = u - jnp.einsum('bhck,bhkv->bhcv', w, s)
        att = jnp.einsum('bhik,bhjk->bhij', qc, kc) * decay
        o = (jnp.einsum('bhik,bhkv->bhiv', qc * jnp.exp(gcum)[..., None], s)
             + jnp.einsum('bhij,bhjv->bhiv', att, v_new))
        g_last = gcum[..., -1:]
        s = (s * jnp.exp(g_last)[..., None]
             + jnp.einsum('bhjk,bhjv->bhkv', kc * jnp.exp(g_last - gcum)[..., None], v_new))
        return s, o

    xs = (_to_chunks(q, c), _to_chunks(k, c), _to_chunks(v, c), _to_chunks(g, c), _to_chunks(beta, c))
    s, o = lax.scan(step, s0, xs)
    return _from_chunks(o), s


def _mixer(h, s_ret, s_gdn, conv_buf, offset, w_in, conv_w, a_log, dt_bias,
           ret_gn_w, ret_gn_b, gdn_norm_w, w_out):
    f32 = jnp.float32
    b, t, _ = h.shape
    p = jnp.matmul(h, w_in).astype(f32)
    q_r, k_r, v_r, g_r, qkv_d, z_d, a_d, b_d = jnp.split(p, IN_SPLITS, axis=-1)

    q_r = _rope(q_r.reshape(b, t, RET_HEADS, RET_DK), offset)
    k_r = _rope(k_r.reshape(b, t, RET_HEADS, RET_DK), offset) * (RET_DK ** -0.5)
    o_r, s_ret_new = _retention(q_r, k_r, v_r.reshape(b, t, RET_HEADS, RET_DV), s_ret.astype(f32))
    mu = jnp.mean(o_r, axis=-1, keepdims=True)
    var = jnp.mean(jnp.square(o_r - mu), axis=-1, keepdims=True)
    o_r = ((o_r - mu) * lax.rsqrt(var + LN_EPS)).reshape(b, t, RET_WIDTH)
    o_r = jax.nn.silu(g_r) * (o_r * ret_gn_w.astype(f32) + ret_gn_b.astype(f32))

    xc = jnp.concatenate([conv_buf.astype(f32), qkv_d], axis=1)
    cw = conv_w.astype(f32)
    conv = xc[:, 0:t] * cw[0]
    for i in range(1, CONV_W):
        conv = conv + xc[:, i:i + t] * cw[i]
    conv = jax.nn.silu(conv)
    conv_new = xc[:, t:]
    q_d, k_d, v_d = jnp.split(conv, [GDN_HEADS * GDN_DK, 2 * GDN_HEADS * GDN_DK], axis=-1)
    q_d = _l2norm(q_d.reshape(b, t, GDN_HEADS, GDN_DK)) * (GDN_DK ** -0.5)
    k_d = _l2norm(k_d.reshape(b, t, GDN_HEADS, GDN_DK))
    v_d = v_d.reshape(b, t, GDN_HEADS, GDN_DV)
    g = -jnp.exp(a_log.astype(f32)) * jax.nn.softplus(a_d + dt_bias.astype(f32))
    beta = jax.nn.sigmoid(b_d)
    o_d, s_gdn_new = _gated_delta(q_d, k_d, v_d, g, beta, s_gdn.astype(f32))
    o_d = o_d * lax.rsqrt(jnp.mean(jnp.square(o_d), axis=-1, keepdims=True) + NORM_EPS) * gdn_norm_w.astype(f32)
    o_d = (o_d * jax.nn.silu(z_d.reshape(b, t, GDN_HEADS, GDN_DV))).reshape(b, t, GDN_WIDTH)

    o = jnp.concatenate([o_r, o_d], axis=-1).astype(h.dtype)
    y = jnp.matmul(o, w_out)
    return y, s_ret_new.astype(h.dtype), s_gdn_new.astype(h.dtype), conv_new.astype(h.dtype)


def _moe(x, w_router, b_router, w_gu, b_gu, w_down, b_down):
    n, d = x.shape
    logits = jnp.matmul(x, w_router).astype(jnp.float32) + b_router.astype(jnp.float32)
    top_val, top_idx = lax.top_k(logits, TOP_K)
    gates = jax.nn.softmax(top_val, axis=-1)
    nk = n * TOP_K
    flat_e = top_idx.reshape(-1).astype(jnp.int32)
    flat_tok = jnp.arange(nk, dtype=jnp.int32) // TOP_K
    flat_gate = gates.reshape(-1)
    order = jnp.argsort(flat_e)
    sorted_e = flat_e[order]
    counts = jnp.zeros((N_EXPERTS,), jnp.int32).at[flat_e].add(1)
    padded = (counts + MOE_BLOCK - 1) // MOE_BLOCK * MOE_BLOCK
    start = jnp.cumsum(counts) - counts
    pad_end = jnp.cumsum(padded)
    pad_start = pad_end - padded
    dest = pad_start[sorted_e] + jnp.arange(nk, dtype=jnp.int32) - start[sorted_e]
    n_blocks = -(-nk // MOE_BLOCK) + N_EXPERTS
    rows = n_blocks * MOE_BLOCK
    row_tok = jnp.full((rows,), n, jnp.int32).at[dest].set(flat_tok[order])
    row_gate = jnp.zeros((rows,), jnp.float32).at[dest].set(flat_gate[order])
    block_start = jnp.arange(n_blocks, dtype=jnp.int32) * MOE_BLOCK
    block_e = jnp.minimum(jnp.searchsorted(pad_end, block_start, side='right'), N_EXPERTS - 1)
    xpad = jnp.concatenate([x, jnp.zeros((1, d), x.dtype)], axis=0)
    xb = xpad[row_tok].reshape(n_blocks, MOE_BLOCK, d)

    def expert_block(args):
        xblk, e = args
        gu = jnp.matmul(xblk, w_gu[e]) + b_gu[e]
        gate, up = gu[:, :D_EXPERT], gu[:, D_EXPERT:]
        gate = jnp.minimum(gate, SWIGLU_LIMIT)
        up = jnp.clip(up, -SWIGLU_LIMIT, SWIGLU_LIMIT)
        act = (up + 1.0) * gate * jax.nn.sigmoid(SWIGLU_ALPHA * gate)
        return jnp.matmul(act, w_down[e]) + b_down[e]

    yb = lax.map(expert_block, (xb, block_e)).reshape(rows, d)
    y = jnp.zeros((n + 1, d), yb.dtype).at[row_tok].add(yb * row_gate[:, None].astype(yb.dtype))
    return y[:n]


def setup_inputs(seed: int = 0) -> dict:
    key = jax.random.key(seed)
    ks = jax.random.split(key, 24)
    f32 = jnp.float32

    def nrm(k, shape, scale):
        return scale * jax.random.normal(k, shape, f32)

    dt = jnp.exp(jax.random.uniform(ks[9], (DEPTH, GDN_HEADS), f32) * (np.log(0.1) - np.log(0.001)) + np.log(0.001))
    return {
        'x_prompt': nrm(ks[0], (BATCH, SEQ, D_MODEL), 1.0),
        'x_sample': nrm(ks[1], (DEC_BATCH, DEC_SEQ, D_MODEL), 1.0),
        'state_ret': nrm(ks[2], (DEPTH, DEC_BATCH, RET_HEADS, RET_DK, RET_DV), 0.5),
        'state_gdn': nrm(ks[3], (DEPTH, DEC_BATCH, GDN_HEADS, GDN_DK, GDN_DV), 0.1),
        'state_conv': nrm(ks[4], (DEPTH, DEC_BATCH, CONV_W - 1, CONV_CH), 1.0),
        'w_in': nrm(ks[5], (DEPTH, D_MODEL, IN_WIDTH), D_MODEL ** -0.5),
        'conv_w': nrm(ks[6], (DEPTH, CONV_W, CONV_CH), CONV_W ** -0.5),
        'a_log': jnp.log(jax.random.uniform(ks[7], (DEPTH, GDN_HEADS), f32, 1.0, 16.0)),
        'dt_bias': dt + jnp.log(-jnp.expm1(-dt)),
        'ret_gn_w': 1.0 + nrm(ks[10], (DEPTH, RET_WIDTH), 0.01),
        'ret_gn_b': nrm(ks[11], (DEPTH, RET_WIDTH), 0.01),
        'gdn_norm_w': 1.0 + nrm(ks[12], (DEPTH, GDN_DV), 0.01),
        'w_out': nrm(ks[13], (DEPTH, D_MODEL, D_MODEL), DEEPNORM_BETA * D_MODEL ** -0.5),
        'ln1_g': 1.0 + nrm(ks[14], (DEPTH, D_MODEL), 0.01),
        'ln1_b': nrm(ks[15], (DEPTH, D_MODEL), 0.01),
        'w_router': nrm(ks[16], (DEPTH, D_MODEL, N_EXPERTS), D_MODEL ** -0.5),
        'b_router': nrm(ks[17], (DEPTH, N_EXPERTS), 0.01),
        'w_gu': nrm(ks[18], (DEPTH, N_EXPERTS, D_MODEL, 2 * D_EXPERT), D_MODEL ** -0.5),
        'b_gu': nrm(ks[19], (DEPTH, N_EXPERTS, 2 * D_EXPERT), 0.01),
        'w_down': nrm(ks[20], (DEPTH, N_EXPERTS, D_EXPERT, D_MODEL), DEEPNORM_BETA * D_EXPERT ** -0.5),
        'b_down': nrm(ks[21], (DEPTH, N_EXPERTS, D_MODEL), 0.01),
        'ln2_g': 1.0 + nrm(ks[22], (DEPTH, D_MODEL), 0.01),
        'ln2_b': nrm(ks[23], (DEPTH, D_MODEL), 0.01),
    }


def reference(x_prompt, x_sample, state_ret, state_gdn, state_conv, w_in, conv_w, a_log, dt_bias,
              ret_gn_w, ret_gn_b, gdn_norm_w, w_out, ln1_g, ln1_b, w_router, b_router,
              w_gu, b_gu, w_down, b_down, ln2_g, ln2_b):
    xp, xs = x_prompt, x_sample
    bp, tp, _ = xp.shape
    bs, ts, _ = xs.shape
    zero_ret = jnp.zeros((bp, RET_HEADS, RET_DK, RET_DV), jnp.float32)
    zero_gdn = jnp.zeros((bp, GDN_HEADS, GDN_DK, GDN_DV), jnp.float32)
    zero_conv = jnp.zeros((bp, CONV_W - 1, CONV_CH), jnp.float32)
    ret_p, gdn_p, conv_p, ret_s, gdn_s, conv_s = [], [], [], [], [], []
    for l in range(DEPTH):
        mp, sr, sg, sc = _mixer(xp, zero_ret, zero_gdn, zero_conv, 0, w_in[l], conv_w[l], a_log[l],
                                dt_bias[l], ret_gn_w[l], ret_gn_b[l], gdn_norm_w[l], w_out[l])
        ret_p.append(sr)
        gdn_p.append(sg)
        conv_p.append(sc)
        ms, sr, sg, sc = _mixer(xs, state_ret[l], state_gdn[l], state_conv[l], PAST_LEN, w_in[l], conv_w[l],
                                a_log[l], dt_bias[l], ret_gn_w[l], ret_gn_b[l], gdn_norm_w[l], w_out[l])
        ret_s.append(sr)
        gdn_s.append(sg)
        conv_s.append(sc)
        xp = _layer_norm(DEEPNORM_ALPHA * xp + mp, ln1_g[l], ln1_b[l])
        xs = _layer_norm(DEEPNORM_ALPHA * xs + ms, ln1_g[l], ln1_b[l])
        h = jnp.concatenate([xp.reshape(-1, D_MODEL), xs.reshape(-1, D_MODEL)], axis=0)
        f = _moe(h, w_router[l], b_router[l], w_gu[l], b_gu[l], w_down[l], b_down[l])
        h = _layer_norm(DEEPNORM_ALPHA * h + f, ln2_g[l], ln2_b[l])
        xp = h[:bp * tp].reshape(bp, tp, D_MODEL)
        xs = h[bp * tp:].reshape(bs, ts, D_MODEL)
    return (xp, xs, jnp.stack(ret_p), jnp.stack(gdn_p), jnp.stack(conv_p),
            jnp.stack(ret_s), jnp.stack(gdn_s), jnp.stack(conv_s))
```

```python
import functools

import jax
import jax.numpy as jnp
from jax import lax
from jax.experimental import pallas as pl
from jax.experimental.pallas import tpu as pltpu

F32 = jnp.float32
BF16 = jnp.bfloat16
I32 = jnp.int32

HEADS = 4
DH = 128
HW = HEADS * DH
CONV_W = 4
ROPE_BASE = 10000.0
TOP_K = 4
SWIGLU_LIMIT = 7.0
SWIGLU_ALPHA = 1.702
LN_EPS = 1e-5
NORM_EPS = 1e-6
L2_EPS = 1e-6

LANES = 128
SUBLANES = 8
VMEM_LIMIT_BYTES = 56 * 1024 * 1024

CHUNK = 128
MOE_BM = 256
TOK_TILE = 128
NEG = -3.0e38

COL_QR, COL_KR, COL_VR, COL_GR, COL_QD, COL_KD, COL_VD, COL_ZD = range(8)
TAIL_COL = 8 * HW
IN_PAD = TAIL_COL + LANES

_NT = (((1,), (1,)), ((), ()))


def _cparams(sem, vmem=VMEM_LIMIT_BYTES):
    return pltpu.CompilerParams(dimension_semantics=sem, vmem_limit_bytes=vmem)


def _dot(a, b, precision=None):
    return jnp.dot(a, b, preferred_element_type=F32, precision=precision)


def _dot_nt(a, b):
    return lax.dot_general(a, b, _NT, preferred_element_type=F32)


def _silu(x):
    return x * jax.nn.sigmoid(x)


def _layer_norm(h, g, b):
    mu = jnp.mean(h, axis=-1, keepdims=True)
    d = h - mu
    var = jnp.mean(d * d, axis=-1, keepdims=True)
    return d * lax.rsqrt(var + LN_EPS) * g + b


def _rope(x, cos2, sin2):
    return x * cos2 + pltpu.roll(x, DH // 2, 1) * sin2


def _pick_tile(n, candidates):
    for c in candidates:
        if n % c == 0:
            return c
    raise ValueError(f"no tile in {candidates} divides {n}")


def _inproj_body(x_ref, w_ref, o_ref):
    o_ref[...] = _dot(x_ref[...].astype(BF16), w_ref[...])


def _inproj(x, w_bf, tm):
    n, d = x.shape
    return pl.pallas_call(
        _inproj_body,
        out_shape=jax.ShapeDtypeStruct((n, IN_PAD), F32),
        grid=(n // tm,),
        in_specs=[pl.BlockSpec((tm, d), lambda i: (i, 0)),
                  pl.BlockSpec((d, IN_PAD), lambda i: (0, 0))],
        out_specs=pl.BlockSpec((tm, IN_PAD), lambda i: (i, 0)),
        compiler_params=_cparams(("arbitrary",)),
        name="inproj",
    )(x, w_bf)


def _ret_body(q_ref, k_ref, v_ref, g_ref, o_in_ref, cos_ref, sin_ref, intra_ref, qdec_ref, kdec_ref, cdec_ref,
              gnw_ref, gnb_ref, o_ref, sout_ref, s_scr, *, nchunk):
    del o_in_ref
    tb = pl.program_id(1)

    @pl.when(tb == 0)
    def _():
        s_scr[...] = jnp.zeros_like(s_scr)

    def chunk(ci, carry):
        rows = pl.ds(pl.multiple_of(ci * CHUNK, CHUNK), CHUNK)
        cos2 = cos_ref[rows, :]
        sin2 = sin_ref[rows, :]
        for h in range(HEADS):
            cs = slice(h * DH, (h + 1) * DH)
            q = _rope(q_ref[rows, cs], cos2, sin2)
            k = _rope(k_ref[rows, cs], cos2, sin2) * (DH ** -0.5)
            vb = v_ref[rows, cs].astype(BF16)
            s = s_scr[h]
            att = _dot_nt(q.astype(BF16), k.astype(BF16)) * intra_ref[h]
            o = _dot(att.astype(BF16), vb) + _dot((q * qdec_ref[h]).astype(BF16), s.astype(BF16))
            kd_t = (k * kdec_ref[h]).T.astype(BF16)
            s_scr[h] = s * cdec_ref[h] + _dot(kd_t, vb)
            mu = jnp.mean(o, axis=-1, keepdims=True)
            d = o - mu
            var = jnp.mean(d * d, axis=-1, keepdims=True)
            on = d * lax.rsqrt(var + LN_EPS)
            res = _silu(g_ref[rows, cs]) * (on * gnw_ref[:, cs] + gnb_ref[:, cs])
            o_ref[rows, cs] = res.astype(o_ref.dtype)
        return carry

    lax.fori_loop(0, nchunk, chunk, 0)

    @pl.when(tb == pl.num_programs(1) - 1)
    def _():
        sout_ref[0] = s_scr[...]


def _ret_tables(c):
    log_gamma = jnp.log(1.0 - 2.0 ** (-5.0 - jnp.arange(HEADS, dtype=F32)))
    idx = jnp.arange(c, dtype=F32)
    diff = idx[:, None] - idx[None, :]
    causal = diff >= 0
    intra = jnp.where(causal, jnp.exp(log_gamma[:, None, None] * jnp.where(causal, diff, 0.0)), 0.0)
    q_dec = jnp.exp(log_gamma[:, None] * (idx + 1.0))[..., None]
    k_dec = jnp.exp(log_gamma[:, None] * (c - 1.0 - idx))[..., None]
    c_dec = jnp.exp(log_gamma * c)[:, None, None]
    bc = lambda a, r: jnp.broadcast_to(a, (HEADS, r, LANES)).astype(F32)
    return intra, bc(q_dec, c), bc(k_dec, c), bc(c_dec, 1)


def _rope_tables(t, offset):
    half = DH // 2
    inv_freq = ROPE_BASE ** (-jnp.arange(half, dtype=F32) / half)
    pos = jnp.arange(t, dtype=F32) + offset
    ang = pos[:, None] * inv_freq[None, :]
    cos, sin = jnp.cos(ang), jnp.sin(ang)
    return jnp.concatenate([cos, cos], -1), jnp.concatenate([-sin, sin], -1)


def _retention_prompt(p, n, bsz, seq, tt, gnw, gnb):
    nchunk = tt // CHUNK
    nt = seq // tt
    cos2, sin2 = _rope_tables(seq, 0)
    intra, qdec, kdec, cdec = _ret_tables(CHUNK)
    col = lambda c: pl.BlockSpec((tt, HW), lambda b, t, c=c: (b * nt + t, c))
    full = lambda a: pl.BlockSpec(a.shape, lambda b, t: (0,) * a.ndim)
    return pl.pallas_call(
        functools.partial(_ret_body, nchunk=nchunk),
        out_shape=(jax.ShapeDtypeStruct((n, 2 * HW), BF16),
                   jax.ShapeDtypeStruct((bsz, HEADS, DH, DH), F32)),
        grid=(bsz, nt),
        in_specs=[col(COL_QR), col(COL_KR), col(COL_VR), col(COL_GR),
                  pl.BlockSpec(memory_space=pl.ANY),
                  pl.BlockSpec((tt, DH), lambda b, t: (t, 0)),
                  pl.BlockSpec((tt, DH), lambda b, t: (t, 0)),
                  full(intra), full(qdec), full(kdec), full(cdec), full(gnw), full(gnb)],
        out_specs=(pl.BlockSpec((tt, HW), lambda b, t: (b * nt + t, 0)),
                   pl.BlockSpec((1, HEADS, DH, DH), lambda b, t: (b, 0, 0, 0))),
        scratch_shapes=[pltpu.VMEM((HEADS, DH, DH), F32)],
        input_output_aliases={4: 0},
        compiler_params=_cparams(("arbitrary", "arbitrary")),
        name="retention_prompt",
    )(p, p, p, p, jnp.zeros((n, 2 * HW), BF16), cos2, sin2, intra, qdec, kdec, cdec, gnw, gnb)


def _softplus(x):
    return jnp.maximum(x, 0.0) + jnp.log1p(jnp.exp(-jnp.abs(x)))


def _gdn_body(qd_ref, kd_ref, vd_ref, z_ref, tail_ref, o_in_ref, cw_ref, alog_ref, dtb_ref, nw_ref,
              o_ref, sout_ref, s_scr, prev_scr, xe_scr, qn_scr, kn_scr, vn_scr, g_scr, b_scr, *, nchunk, tt):
    del o_in_ref
    tb = pl.program_id(1)
    hp = lax.Precision.HIGHEST

    @pl.when(tb == 0)
    def _():
        s_scr[...] = jnp.zeros_like(s_scr)
        prev_scr[...] = jnp.zeros_like(prev_scr)

    for gi, (src, dst) in enumerate(((qd_ref, qn_scr), (kd_ref, kn_scr), (vd_ref, vn_scr))):
        gcols = slice(gi * HW, (gi + 1) * HW)
        xe_scr[0:SUBLANES, :] = prev_scr[gi]
        xe_scr[SUBLANES:SUBLANES + tt, :] = src[...]
        prev_scr[gi] = src[tt - SUBLANES:tt, :]
        for c in range(nchunk):
            base = SUBLANES + c * CHUNK
            acc = xe_scr[base:base + CHUNK, :] * cw_ref[CONV_W - 1:CONV_W, gcols]
            for j in range(1, CONV_W):
                acc = acc + xe_scr[base - j:base - j + CHUNK, :] * cw_ref[CONV_W - 1 - j:CONV_W - j, gcols]
            y = _silu(acc)
            for h in range(HEADS):
                cs = slice(h * DH, (h + 1) * DH)
                seg = y[:, cs]
                if gi < 2:
                    seg = seg * lax.rsqrt(jnp.sum(seg * seg, axis=-1, keepdims=True) + L2_EPS)
                if gi == 0:
                    seg = seg * (DH ** -0.5)
                dst[c * CHUNK:(c + 1) * CHUNK, cs] = seg

    tail = tail_ref[...]
    g_scr[...] = -jnp.exp(alog_ref[...]) * _softplus(tail + dtb_ref[...])
    b_scr[...] = jax.nn.sigmoid(tail)

    ri = lax.broadcasted_iota(I32, (CHUNK, CHUNK), 0)
    ci_ = lax.broadcasted_iota(I32, (CHUNK, CHUNK), 1)
    causal = ri >= ci_
    strict = ri > ci_
    ltri = jnp.where(causal, 1.0, 0.0).astype(F32)
    eye = jnp.where(ri == ci_, 1.0, 0.0).astype(F32)

    def chunk(ci, carry):
        rows = pl.ds(pl.multiple_of(ci * CHUNK, CHUNK), CHUNK)
        gcum = _dot(ltri, g_scr[rows, :], hp)
        gcum_t = gcum.T
        bt = b_scr[rows, :]
        for h in range(HEADS):
            cs = slice(h * DH, (h + 1) * DH)
            gc = gcum[:, h:h + 1]
            gr = gcum_t[h:h + 1, :]
            beta = bt[:, HEADS + h:HEADS + h + 1]
            decay = jnp.where(causal, jnp.exp(jnp.where(causal, gc - gr, 0.0)), 0.0)
            q = qn_scr[rows, cs]
            k = kn_scr[rows, cs]
            v = vn_scr[rows, cs]
            kb = k.astype(BF16)
            kk = _dot_nt(kb, kb)
            nm = jnp.where(strict, beta * kk * decay, 0.0)
            tinv = eye - nm
            pw = nm
            for _ in range(6):
                pw = _dot(pw, pw, hp)
                tinv = tinv + _dot(tinv, pw, hp)
            egc = jnp.exp(gc)
            u = _dot(tinv, v * beta, hp)
            w = _dot(tinv, k * (beta * egc), hp)
            s = s_scr[h]
            sb = s.astype(BF16)
            v_new = u - _dot(w.astype(BF16), sb)
            vnb = v_new.astype(BF16)
            att = _dot_nt(q.astype(BF16), kb) * decay
            o = _dot((q * egc).astype(BF16), sb) + _dot(att.astype(BF16), vnb)
            g_last = gcum[CHUNK - 1:CHUNK, h:h + 1]
            kd_t = (k * jnp.exp(g_last - gc)).T.astype(BF16)
            s_scr[h] = s * jnp.exp(g_last) + _dot(kd_t, vnb)
            o = o * lax.rsqrt(jnp.mean(o * o, axis=-1, keepdims=True) + NORM_EPS) * nw_ref[...]
            o = o * _silu(z_ref[rows, cs])
            o_ref[rows, cs] = o.astype(o_ref.dtype)
        return carry

    lax.fori_loop(0, nchunk, chunk, 0)

    @pl.when(tb == pl.num_programs(1) - 1)
    def _():
        sout_ref[0] = s_scr[...]


def _gdn_prompt(p, o, bsz, seq, tt, cw, alog_row, dtb_row, nw):
    nchunk = tt // CHUNK
    nt = seq // tt
    n = p.shape[0]
    col = lambda c: pl.BlockSpec((tt, HW), lambda b, t, c=c: (b * nt + t, c))
    full = lambda a: pl.BlockSpec(a.shape, lambda b, t: (0,) * a.ndim)
    return pl.pallas_call(
        functools.partial(_gdn_body, nchunk=nchunk, tt=tt),
        out_shape=(jax.ShapeDtypeStruct((n, 2 * HW), BF16),
                   jax.ShapeDtypeStruct((bsz, HEADS, DH, DH), F32)),
        grid=(bsz, nt),
        in_specs=[col(COL_QD), col(COL_KD), col(COL_VD), col(COL_ZD),
                  pl.BlockSpec((tt, LANES), lambda b, t: (b * nt + t, TAIL_COL // LANES)),
                  pl.BlockSpec(memory_space=pl.ANY),
                  full(cw), full(alog_row), full(dtb_row), full(nw)],
        out_specs=(pl.BlockSpec((tt, HW), lambda b, t: (b * nt + t, 1)),
                   pl.BlockSpec((1, HEADS, DH, DH), lambda b, t: (b, 0, 0, 0))),
        scratch_shapes=[pltpu.VMEM((HEADS, DH, DH), F32),
                        pltpu.VMEM((3, SUBLANES, HW), F32),
                        pltpu.VMEM((SUBLANES + tt, HW), F32),
                        pltpu.VMEM((tt, HW), F32), pltpu.VMEM((tt, HW), F32), pltpu.VMEM((tt, HW), F32),
                        pltpu.VMEM((tt, LANES), F32), pltpu.VMEM((tt, LANES), F32)],
        input_output_aliases={5: 0},
        compiler_params=_cparams(("arbitrary", "arbitrary")),
        name="gdn_prompt",
    )(p, p, p, p, p, o, cw, alog_row, dtb_row, nw)


def _to_col(row, eye_mask):
    return jnp.sum(jnp.where(eye_mask, row, 0.0), axis=1, keepdims=True)


def _dec_body(qr_ref, kr_ref, vr_ref, gr_ref, qd_ref, kd_ref, vd_ref, zd_ref, tail_ref,
              sret_ref, sgdn_ref, sconv_ref, o_in_ref, cos_ref, sin_ref, gam_ref, cw_ref, alog_ref, dtb_ref,
              gnw_ref, gnb_ref, nw_ref,
              o_ref, sret_o, sgdn_o, conv_o,
              qr_s, kr_s, qd_s, kd_s, vd_s, gb_s, o_scr, *, bb):
    del o_in_ref
    step = pl.program_id(0)

    @pl.when(step == 0)
    def _():
        cos2 = cos_ref[...]
        sin2 = sin_ref[...]
        for h in range(HEADS):
            cs = slice(h * DH, (h + 1) * DH)
            qr_s[:, cs] = _rope(qr_ref[:, cs], cos2, sin2)
            kr_s[:, cs] = _rope(kr_ref[:, cs], cos2, sin2) * (DH ** -0.5)
        for gi, (src, dst) in enumerate(((qd_ref, qd_s), (kd_ref, kd_s), (vd_ref, vd_s))):
            gcols = slice(gi * HW, (gi + 1) * HW)
            x = src[...]
            acc = x * cw_ref[CONV_W - 1:CONV_W, gcols]
            for i in range(CONV_W - 1):
                acc = acc + sconv_ref[i, :, gcols] * cw_ref[i:i + 1, gcols]
            for i in range(CONV_W - 2):
                conv_o[i, :, gcols] = sconv_ref[i + 1, :, gcols]
            conv_o[CONV_W - 2, :, gcols] = x
            y = _silu(acc)
            for h in range(HEADS):
                cs = slice(h * DH, (h + 1) * DH)
                seg = y[:, cs]
                if gi < 2:
                    seg = seg * lax.rsqrt(jnp.sum(seg * seg, axis=-1, keepdims=True) + L2_EPS)
                if gi == 0:
                    seg = seg * (DH ** -0.5)
                dst[:, cs] = seg
        tail = tail_ref[...]
        lane = lax.broadcasted_iota(I32, tail.shape, 1)
        g = -jnp.exp(alog_ref[...]) * _softplus(tail + dtb_ref[...])
        gb_s[...] = jnp.where(lane < HEADS, g, jax.nn.sigmoid(tail))

    eye_mask = lax.broadcasted_iota(I32, (DH, DH), 0) == lax.broadcasted_iota(I32, (DH, DH), 1)

    rows8 = pl.ds(pl.multiple_of(step * bb, bb), bb)
    sub = lax.broadcasted_iota(I32, (bb, DH), 0)

    def per_seq(j, o8):
        pick = lambda ref, cs: jnp.sum(jnp.where(sub == j, ref[rows8, cs], 0.0), axis=0, keepdims=True)
        gb = pick(gb_s, slice(0, LANES))
        pieces_r, pieces_d = [], []
        for h in range(HEADS):
            cs = slice(h * DH, (h + 1) * DH)
            s = sret_ref[j, h]
            kcol = _to_col(pick(kr_s, cs), eye_mask)
            qcol = _to_col(pick(qr_s, cs), eye_mask)
            sn = s * gam_ref[h] + kcol * pick(vr_ref, cs)
            sret_o[j, h] = sn
            o = jnp.sum(qcol * sn, axis=0, keepdims=True)
            mu = jnp.mean(o, axis=-1, keepdims=True)
            d = o - mu
            var = jnp.mean(d * d, axis=-1, keepdims=True)
            on = d * lax.rsqrt(var + LN_EPS)
            pieces_r.append(_silu(pick(gr_ref, cs)) * (on * gnw_ref[:, cs] + gnb_ref[:, cs]))
            s = sgdn_ref[j, h]
            eg = jnp.exp(gb[:, h:h + 1])
            beta = gb[:, HEADS + h:HEADS + h + 1]
            kcol = _to_col(pick(kd_s, cs), eye_mask)
            qcol = _to_col(pick(qd_s, cs), eye_mask)
            ks = jnp.sum(kcol * s, axis=0, keepdims=True)
            v_new = beta * (pick(vd_s, cs) - eg * ks)
            sn = s * eg + kcol * v_new
            sgdn_o[j, h] = sn
            o = jnp.sum(qcol * sn, axis=0, keepdims=True)
            o = o * lax.rsqrt(jnp.mean(o * o, axis=-1, keepdims=True) + NORM_EPS) * nw_ref[...]
            pieces_d.append(o * _silu(pick(zd_ref, cs)))
        orow = jnp.concatenate(pieces_r + pieces_d, axis=1)
        return jnp.where(lax.broadcasted_iota(I32, o8.shape, 0) == j, orow, o8)

    o_scr[rows8, :] = lax.fori_loop(0, bb, per_seq, jnp.zeros((bb, 2 * HW), F32))

    @pl.when(step == pl.num_programs(0) - 1)
    def _():
        o_ref[...] = o_scr[...].astype(o_ref.dtype)


def _mixer_sample(p, o, n_prompt, dbs, s_ret, s_gdn, s_conv_t, offset, cw, alog_row, dtb_row, gnw, gnb, nw):
    bb = SUBLANES
    n = p.shape[0]
    rb = n_prompt // dbs
    cos2, sin2 = _rope_tables(1, offset)
    log_gamma = jnp.log(1.0 - 2.0 ** (-5.0 - jnp.arange(HEADS, dtype=F32)))
    gam = jnp.broadcast_to(jnp.exp(log_gamma)[:, None, None], (HEADS, 1, LANES)).astype(F32)
    col = lambda c: pl.BlockSpec((dbs, HW), lambda s, c=c: (rb, c))
    full = lambda a: pl.BlockSpec(a.shape, lambda s: (0,) * a.ndim)
    st = pl.BlockSpec((bb, HEADS, DH, DH), lambda s: (s, 0, 0, 0))
    vec = lambda: pltpu.VMEM((dbs, HW), F32)
    return pl.pallas_call(
        functools.partial(_dec_body, bb=bb),
        out_shape=(jax.ShapeDtypeStruct((n, 2 * HW), BF16),
                   jax.ShapeDtypeStruct(s_ret.shape, F32),
                   jax.ShapeDtypeStruct(s_gdn.shape, F32),
                   jax.ShapeDtypeStruct(s_conv_t.shape, F32)),
        grid=(dbs // bb,),
        in_specs=[col(COL_QR), col(COL_KR), col(COL_VR), col(COL_GR),
                  col(COL_QD), col(COL_KD), col(COL_VD), col(COL_ZD),
                  pl.BlockSpec((dbs, LANES), lambda s: (rb, TAIL_COL // LANES)),
                  st, st, full(s_conv_t),
                  pl.BlockSpec(memory_space=pl.ANY),
                  full(cos2), full(sin2), full(gam), full(cw), full(alog_row), full(dtb_row),
                  full(gnw), full(gnb), full(nw)],
        out_specs=(pl.BlockSpec((dbs, 2 * HW), lambda s: (rb, 0)), st, st, full(s_conv_t)),
        scratch_shapes=[vec(), vec(), vec(), vec(), vec(),
                        pltpu.VMEM((dbs, LANES), F32), pltpu.VMEM((dbs, 2 * HW), F32)],
        input_output_aliases={12: 0},
        compiler_params=_cparams(("arbitrary",)),
        name="mixer_sample",
    )(p, p, p, p, p, p, p, p, p, s_ret, s_gdn, s_conv_t, o, cos2, sin2, gam, cw, alog_row, dtb_row,
      gnw, gnb, nw)


def _post_mixer_body(o_ref, x_ref, wout_ref, g_ref, b_ref, wr_ref, br_ref,
                     x1_ref, e4_ref, r4_ref, g4_ref, cnt_ref, carry_scr, *, alpha, tm):
    i = pl.program_id(0)

    @pl.when(i == 0)
    def _():
        carry_scr[...] = jnp.zeros_like(carry_scr)

    y = _dot(o_ref[...], wout_ref[...])
    x1 = _layer_norm(alpha * x_ref[...] + y, g_ref[...], b_ref[...])
    x1_ref[...] = x1
    logits = _dot(x1.astype(BF16), wr_ref[...]) + br_ref[...]
    lane = lax.broadcasted_iota(I32, (tm, LANES), 1)
    lane_f = lane.astype(F32)
    work = logits
    ohs, vals, idxs = [], [], []
    for _ in range(TOP_K):
        m = jnp.max(work, axis=1, keepdims=True)
        idx = jnp.min(jnp.where(work == m, lane_f, float(LANES)), axis=1, keepdims=True)
        oh = lane_f == idx
        ohs.append(oh)
        vals.append(m)
        idxs.append(idx)
        work = jnp.where(oh, NEG, work)
    es = [jnp.exp(v - vals[0]) for v in vals]
    den = es[0] + es[1] + es[2] + es[3]
    sel = jnp.zeros((tm, LANES), F32)
    for oh in ohs:
        sel = sel + jnp.where(oh, 1.0, 0.0)
    r_i = lax.broadcasted_iota(I32, (tm, tm), 0)
    c_i = lax.broadcasted_iota(I32, (tm, tm), 1)
    ltri = jnp.where(r_i >= c_i, 1.0, 0.0).astype(BF16)
    incl = _dot(ltri, sel.astype(BF16))
    excl = incl - sel + carry_scr[0:1, :]
    carry_scr[...] = carry_scr[...] + incl[tm - 1:tm, :]
    e4 = jnp.zeros((tm, LANES), F32)
    r4 = jnp.zeros((tm, LANES), F32)
    g4 = jnp.zeros((tm, LANES), F32)
    for k in range(TOP_K):
        rk = jnp.sum(jnp.where(ohs[k], excl, 0.0), axis=1, keepdims=True)
        e4 = jnp.where(lane == k, idxs[k], e4)
        r4 = jnp.where(lane == k, rk, r4)
        g4 = jnp.where(lane == k, es[k] / den, g4)
    e4_ref[...] = e4.astype(I32)
    r4_ref[...] = r4.astype(I32)
    g4_ref[...] = g4
    cnt_ref[...] = carry_scr[...]


def _post_mixer(o, x, wout_bf, ln_g, ln_b, wr_bf, br_row, alpha, tm):
    n, d = x.shape
    row = lambda w: pl.BlockSpec((tm, w), lambda i: (i, 0))
    full = lambda a: pl.BlockSpec(a.shape, lambda i: (0,) * a.ndim)
    return pl.pallas_call(
        functools.partial(_post_mixer_body, alpha=alpha, tm=tm),
        out_shape=(jax.ShapeDtypeStruct((n, d), F32),
                   jax.ShapeDtypeStruct((n, LANES), I32),
                   jax.ShapeDtypeStruct((n, LANES), I32),
                   jax.ShapeDtypeStruct((n, LANES), F32),
                   jax.ShapeDtypeStruct((SUBLANES, LANES), F32)),
        grid=(n // tm,),
        in_specs=[row(2 * HW), row(d), full(wout_bf), full(ln_g), full(ln_b), full(wr_bf), full(br_row)],
        out_specs=(row(d), row(LANES), row(LANES), row(LANES),
                   pl.BlockSpec((SUBLANES, LANES), lambda i: (0, 0))),
        scratch_shapes=[pltpu.VMEM((SUBLANES, LANES), F32)],
        compiler_params=_cparams(("arbitrary",)),
        name="post_mixer",
    )(o, x, wout_bf, ln_g, ln_b, wr_bf, br_row)


def _plan_body(e4_ref, r4_ref, cnt_ref, d4_ref, blk_ref, nused_ref, *, tm, n_experts, nbp):
    hp = lax.Precision.HIGHEST
    cnt = cnt_ref[...]
    padded = jnp.floor((cnt + (MOE_BM - 1.0)) * (1.0 / MOE_BM)) * MOE_BM
    m_i = lax.broadcasted_iota(I32, (LANES, LANES), 0)
    j_i = lax.broadcasted_iota(I32, (LANES, LANES), 1)
    upper = jnp.where(m_i <= j_i, 1.0, 0.0).astype(F32)
    pad_end = _dot(padded, upper, hp)
    pad_start = (pad_end - padded)[0:1, :]

    lane = lax.broadcasted_iota(I32, (tm, LANES), 1)
    e4 = e4_ref[...]
    r4 = r4_ref[...].astype(F32)
    d4 = jnp.zeros((tm, LANES), F32)
    for k in range(TOP_K):
        ek = e4[:, k:k + 1]
        ps = jnp.sum(jnp.where(lane == ek, pad_start, 0.0), axis=1, keepdims=True)
        d4 = jnp.where(lane == k, ps + r4[:, k:k + 1], d4)
    d4_ref[...] = d4.astype(I32)

    bstart = lax.broadcasted_iota(I32, (nbp, LANES), 0).astype(F32) * MOE_BM
    blane = lax.broadcasted_iota(I32, (nbp, LANES), 1)
    hit = jnp.where((pad_end[0:1, :] <= bstart) & (blane < n_experts), 1.0, 0.0)
    be = jnp.minimum(jnp.sum(hit, axis=1, keepdims=True), n_experts - 1.0)
    blk_ref[...] = jnp.broadcast_to(be, (nbp, LANES)).astype(I32)
    total = jnp.sum(jnp.where(lax.broadcasted_iota(I32, (SUBLANES, LANES), 1) == n_experts - 1, pad_end, 0.0),
                    axis=1, keepdims=True)
    nused_ref[...] = jnp.broadcast_to(total * (1.0 / MOE_BM), (SUBLANES, LANES)).astype(I32)


def _plan(e4, r4, cnt, tm, n_experts, nbp):
    n = e4.shape[0]
    row = pl.BlockSpec((tm, LANES), lambda i: (i, 0))
    return pl.pallas_call(
        functools.partial(_plan_body, tm=tm, n_experts=n_experts, nbp=nbp),
        out_shape=(jax.ShapeDtypeStruct((n, LANES), I32),
                   jax.ShapeDtypeStruct((nbp, LANES), I32),
                   jax.ShapeDtypeStruct((SUBLANES, LANES), I32)),
        grid=(n // tm,),
        in_specs=[row, row, pl.BlockSpec((SUBLANES, LANES), lambda i: (0, 0))],
        out_specs=(row, pl.BlockSpec((nbp, LANES), lambda i: (0, 0)),
                   pl.BlockSpec((SUBLANES, LANES), lambda i: (0, 0))),
        compiler_params=_cparams(("arbitrary",)),
        name="moe_plan",
    )(e4, r4, cnt)


def _dispatch_body(dest_ref, x_ref, xs_in_ref, xs_ref, sem):
    del xs_in_ref
    npairs = TOK_TILE * TOP_K

    def row_copy(pidx):
        t = lax.shift_right_logical(pidx, 2)
        d = dest_ref[0, 0, pidx]
        return pltpu.make_async_copy(x_ref.at[pl.ds(t, 1), :], xs_ref.at[pl.ds(d, 1), :], sem)

    def start(pidx, carry):
        row_copy(pidx).start()
        return carry

    def wait(pidx, carry):
        row_copy(pidx).wait()
        return carry

    lax.fori_loop(0, npairs, start, 0)
    lax.fori_loop(0, npairs, wait, 0)


def _dispatch(x1, dest3, rows):
    n, d = x1.shape
    xs0 = jnp.zeros((rows, d), F32)
    return pl.pallas_call(
        _dispatch_body,
        out_shape=jax.ShapeDtypeStruct((rows, d), F32),
        grid=(n // TOK_TILE,),
        in_specs=[pl.BlockSpec((1, 1, TOK_TILE * TOP_K), lambda i: (i, 0, 0), memory_space=pltpu.SMEM),
                  pl.BlockSpec((TOK_TILE, d), lambda i: (i, 0)),
                  pl.BlockSpec(memory_space=pl.ANY)],
        out_specs=pl.BlockSpec(memory_space=pl.ANY),
        scratch_shapes=[pltpu.SemaphoreType.DMA(())],
        input_output_aliases={2: 0},
        compiler_params=_cparams(("arbitrary",)),
        name="moe_dispatch",
    )(dest3, x1, xs0)


def _ffn_body(blk_ref, nused_ref, xs_ref, wgu_ref, bgu_ref, wd_ref, bd_ref, ys_ref, wgu_bf, wd_bf, *, de):
    i = pl.program_id(0)
    e = blk_ref[i]
    prev = blk_ref[jnp.maximum(i - 1, 0)]

    @pl.when((i == 0) | (e != prev))
    def _():
        rb = 128
        for r in range(0, wgu_ref.shape[0], rb):
            wgu_bf[r:r + rb, :] = wgu_ref[r:r + rb, :].astype(BF16)
        for r in range(0, wd_ref.shape[0], rb):
            wd_bf[r:r + rb, :] = wd_ref[r:r + rb, :].astype(BF16)

    @pl.when(i < nused_ref[0])
    def _():
        gu = _dot(xs_ref[...].astype(BF16), wgu_bf[...]) + bgu_ref[...]
        gate = jnp.minimum(gu[:, :de], SWIGLU_LIMIT)
        up = jnp.clip(gu[:, de:], -SWIGLU_LIMIT, SWIGLU_LIMIT)
        act = (up + 1.0) * gate * jax.nn.sigmoid(SWIGLU_ALPHA * gate)
        ys_ref[...] = _dot(act.astype(BF16), wd_bf[...]) + bd_ref[...]

    @pl.when(i >= nused_ref[0])
    def _():
        ys_ref[...] = jnp.zeros_like(ys_ref)


def _ffn(xs, blk_e, nused, w_gu, b_gu, w_down, b_down):
    rows, d = xs.shape
    n_experts, _, de2 = w_gu.shape
    de = de2 // 2
    nb = rows // MOE_BM
    grid_spec = pltpu.PrefetchScalarGridSpec(
        num_scalar_prefetch=2,
        grid=(nb,),
        in_specs=[pl.BlockSpec((MOE_BM, d), lambda i, blk, nu: (i, 0)),
                  pl.BlockSpec((None, d, de2), lambda i, blk, nu: (blk[i], 0, 0)),
                  pl.BlockSpec((None, 1, de2), lambda i, blk, nu: (blk[i], 0, 0)),
                  pl.BlockSpec((None, de, d), lambda i, blk, nu: (blk[i], 0, 0)),
                  pl.BlockSpec((None, 1, d), lambda i, blk, nu: (blk[i], 0, 0))],
        out_specs=pl.BlockSpec((MOE_BM, d), lambda i, blk, nu: (i, 0)),
        scratch_shapes=[pltpu.VMEM((d, de2), BF16), pltpu.VMEM((de, d), BF16)],
    )
    return pl.pallas_call(
        functools.partial(_ffn_body, de=de),
        out_shape=jax.ShapeDtypeStruct((rows, d), F32),
        grid_spec=grid_spec,
        compiler_params=_cparams(("arbitrary",)),
        name="moe_ffn",
    )(blk_e, nused, xs, w_gu, b_gu.reshape(n_experts, 1, de2), w_down, b_down.reshape(n_experts, 1, d))


def _combine_body(dest_ref, g4_ref, x1_ref, g_ref, b_ref, ys_ref, o_ref, buf, sem, *, alpha):
    npairs = TOK_TILE * TOP_K

    def row_copy(pidx):
        t = lax.shift_right_logical(pidx, 2)
        k = lax.bitwise_and(pidx, TOP_K - 1)
        d = dest_ref[0, 0, pidx]
        return pltpu.make_async_copy(ys_ref.at[pl.ds(d, 1), :], buf.at[k, pl.ds(t, 1), :], sem)

    def start(pidx, carry):
        row_copy(pidx).start()
        return carry

    def wait(pidx, carry):
        row_copy(pidx).wait()
        return carry

    lax.fori_loop(0, npairs, start, 0)
    lax.fori_loop(0, npairs, wait, 0)
    g4 = g4_ref[...]
    f = g4[:, 0:1] * buf[0]
    for k in range(1, TOP_K):
        f = f + g4[:, k:k + 1] * buf[k]
    o_ref[...] = _layer_norm(alpha * x1_ref[...] + f, g_ref[...], b_ref[...])


def _combine(ys, dest3, g4, x1, ln_g, ln_b, alpha):
    n, d = x1.shape
    full = lambda a: pl.BlockSpec(a.shape, lambda i: (0,) * a.ndim)
    return pl.pallas_call(
        functools.partial(_combine_body, alpha=alpha),
        out_shape=jax.ShapeDtypeStruct((n, d), F32),
        grid=(n // TOK_TILE,),
        in_specs=[pl.BlockSpec((1, 1, TOK_TILE * TOP_K), lambda i: (i, 0, 0), memory_space=pltpu.SMEM),
                  pl.BlockSpec((TOK_TILE, LANES), lambda i: (i, 0)),
                  pl.BlockSpec((TOK_TILE, d), lambda i: (i, 0)),
                  full(ln_g), full(ln_b),
                  pl.BlockSpec(memory_space=pl.ANY)],
        out_specs=pl.BlockSpec((TOK_TILE, d), lambda i: (i, 0)),
        scratch_shapes=[pltpu.VMEM((TOP_K, TOK_TILE, d), F32), pltpu.SemaphoreType.DMA(())],
        compiler_params=_cparams(("arbitrary",)),
        name="moe_combine",
    )(dest3, g4, x1, ln_g, ln_b, ys)


def kernel(x_prompt, x_sample, state_ret, state_gdn, state_conv, w_in, conv_w, a_log, dt_bias, ret_gn_w, ret_gn_b, gdn_norm_w, w_out, ln1_g, ln1_b, w_router, b_router, w_gu, b_gu, w_down, b_down, ln2_g, ln2_b):
    bsz, seq, d = x_prompt.shape
    dbs, dseq, _ = x_sample.shape
    depth = w_in.shape[0]
    n_experts = w_router.shape[-1]
    assert dseq == 1 and dbs == LANES and seq % CHUNK == 0 and d == 2 * HW
    past_len = 16384
    n_prompt = bsz * seq
    n = n_prompt + dbs
    alpha = (2.0 * depth) ** 0.25
    tm = _pick_tile(n, (384, 256, 128))
    tt = _pick_tile(seq, (512, 256, 128))
    rows = (pl.cdiv(n * TOP_K, MOE_BM) + n_experts) * MOE_BM
    nbp = pl.cdiv(rows // MOE_BM, SUBLANES) * SUBLANES

    x = jnp.concatenate([x_prompt.reshape(n_prompt, d), x_sample.reshape(dbs, d)], axis=0)
    row2 = lambda a: a.reshape(1, -1).astype(F32)
    pad_lanes = lambda a, fill: jnp.concatenate(
        [a.astype(F32), jnp.full((LANES - a.shape[0],), fill, F32)]).reshape(1, LANES)

    ret_p, gdn_p, conv_p, ret_s, gdn_s, conv_s = [], [], [], [], [], []
    for l in range(depth):
        w_in_bf = jnp.pad(w_in[l], ((0, 0), (0, IN_PAD - w_in.shape[-1]))).astype(BF16)
        p = _inproj(x, w_in_bf, tm)
        gnw, gnb, nw = row2(ret_gn_w[l]), row2(ret_gn_b[l]), row2(gdn_norm_w[l])
        alog_row, dtb_row = pad_lanes(a_log[l], 0.0), pad_lanes(dt_bias[l], 0.0)
        cw = conv_w[l].astype(F32)

        o, sr = _retention_prompt(p, n, bsz, seq, tt, gnw, gnb)
        o, sg = _gdn_prompt(p, o, bsz, seq, tt, cw, alog_row, dtb_row, nw)
        sconv_t = jnp.transpose(state_conv[l], (1, 0, 2))
        o, sr_s, sg_s, sc_s = _mixer_sample(p, o, n_prompt, dbs, state_ret[l], state_gdn[l], sconv_t,
                                            float(past_len), cw, alog_row, dtb_row, gnw, gnb, nw)
        ret_p.append(sr)
        gdn_p.append(sg)
        conv_p.append(p[:n_prompt].reshape(bsz, seq, IN_PAD)[:, seq - (CONV_W - 1):, COL_QD * HW:COL_ZD * HW])
        ret_s.append(sr_s)
        gdn_s.append(sg_s)
        conv_s.append(jnp.transpose(sc_s, (1, 0, 2)))

        wr_bf = jnp.pad(w_router[l], ((0, 0), (0, LANES - n_experts))).astype(BF16)
        br_row = pad_lanes(b_router[l], NEG)
        x1, e4, r4, g4, cnt = _post_mixer(o, x, w_out[l].astype(BF16), row2(ln1_g[l]), row2(ln1_b[l]),
                                          wr_bf, br_row, alpha, tm)
        d4, blk, nused = _plan(e4, r4, cnt, tm, n_experts, nbp)
        dest3 = d4[:, :TOP_K].reshape(n // TOK_TILE, 1, TOK_TILE * TOP_K)
        xs = _dispatch(x1, dest3, rows)
        ys = _ffn(xs, blk[:rows // MOE_BM, 0], nused[0, :1], w_gu[l], b_gu[l], w_down[l], b_down[l])
        x = _combine(ys, dest3, g4, x1, row2(ln2_g[l]), row2(ln2_b[l]), alpha)

    y_prompt = x[:n_prompt].reshape(bsz, seq, d)
    y_sample = x[n_prompt:].reshape(dbs, dseq, d)
    return (y_prompt, y_sample, jnp.stack(ret_p), jnp.stack(gdn_p), jnp.stack(conv_p),
            jnp.stack(ret_s), jnp.stack(gdn_s), jnp.stack(conv_s))
```

```python
import functools

import jax
import jax.numpy as jnp
from jax import lax
from jax.experimental import pallas as pl
from jax.experimental.pallas import tpu as pltpu

F32 = jnp.float32
BF16 = jnp.bfloat16
I32 = jnp.int32

HEADS = 4
DH = 128
HW = HEADS * DH
CONV_W = 4
ROPE_BASE = 10000.0
PAST_LEN = 16384
TOP_K = 4
SWIGLU_LIMIT = 7.0
SWIGLU_ALPHA = 1.702
LN_EPS = 1e-5
NORM_EPS = 1e-6
L2_EPS = 1e-6

LANES = 128
SUBLANES = 8
VMEM_LIMIT_BYTES = 56 * 1024 * 1024

CHUNK = 128
INV_BASE = 16
INV_BASE_LEVELS = 3
MOE_BM = 256
TOK_TILE = 128
DMA_UNROLL = 8
NEG = -3.0e38

COL_QR, COL_KR, COL_VR, COL_GR, COL_QD, COL_KD, COL_VD, COL_ZD = range(8)
TAIL_COL = 8 * HW
IN_PAD = TAIL_COL + LANES

_NT = (((1,), (1,)), ((), ()))


def _cparams(sem, vmem=VMEM_LIMIT_BYTES):
    return pltpu.CompilerParams(dimension_semantics=sem, vmem_limit_bytes=vmem)


def _dot(a, b, precision=None):
    return jnp.dot(a, b, preferred_element_type=F32, precision=precision)


def _dot_nt(a, b):
    return lax.dot_general(a, b, _NT, preferred_element_type=F32)


def _silu(x):
    return x * jax.nn.sigmoid(x)


def _layer_norm(h, g, b):
    mu = jnp.mean(h, axis=-1, keepdims=True)
    d = h - mu
    var = jnp.mean(d * d, axis=-1, keepdims=True)
    return d * lax.rsqrt(var + LN_EPS) * g + b


def _rope(x, cos2, sin2):
    return x * cos2 + pltpu.roll(x, DH // 2, 1) * sin2


def _tiles_to_rows(ref, nrows, lead=()):
    return jnp.concatenate([ref[lead + (pl.ds(s, nrows, stride=SUBLANES), slice(None))] for s in range(SUBLANES)],
                           axis=1)


def _rows_to_tiles(ref, val):
    nrows = val.shape[0]
    for s in range(SUBLANES):
        ref[pl.ds(s, nrows, stride=SUBLANES), :] = val[:, s * LANES:(s + 1) * LANES]


def _pick_tile(n, candidates):
    for c in candidates:
        if n % c == 0:
            return c
    raise ValueError(f"no tile in {candidates} divides {n}")


def _inproj_body(x_ref, w_ref, o_ref):
    o_ref[...] = _dot(x_ref[...].astype(BF16), w_ref[...])


def _inproj(x, w_bf, tm):
    n, d = x.shape
    return pl.pallas_call(
        _inproj_body,
        out_shape=jax.ShapeDtypeStruct((n, IN_PAD), F32),
        grid=(n // tm,),
        in_specs=[pl.BlockSpec((tm, d), lambda i: (i, 0)),
                  pl.BlockSpec((d, IN_PAD), lambda i: (0, 0))],
        out_specs=pl.BlockSpec((tm, IN_PAD), lambda i: (i, 0)),
        compiler_params=_cparams(("arbitrary",)),
        name="inproj",
    )(x, w_bf)


def _ret_body(q_ref, k_ref, v_ref, g_ref, o_in_ref, cos_ref, sin_ref, intra_ref, qdec_ref, kdec_ref, cdec_ref,
              gnw_ref, gnb_ref, o_ref, sout_ref, s_scr, *, nchunk):
    del o_in_ref
    tb = pl.program_id(1)

    @pl.when(tb == 0)
    def _():
        s_scr[...] = jnp.zeros_like(s_scr)

    def chunk(ci, carry):
        rows = pl.ds(pl.multiple_of(ci * CHUNK, CHUNK), CHUNK)
        cos2 = cos_ref[rows, :]
        sin2 = sin_ref[rows, :]
        hs = range(HEADS)
        cols = [slice(h * DH, (h + 1) * DH) for h in hs]
        q = [_rope(q_ref[rows, cols[h]], cos2, sin2) for h in hs]
        k = [_rope(k_ref[rows, cols[h]], cos2, sin2) * (DH ** -0.5) for h in hs]
        vb = [v_ref[rows, cols[h]].astype(BF16) for h in hs]
        s = [s_scr[h] for h in hs]
        att = [_dot_nt(q[h].astype(BF16), k[h].astype(BF16)) * intra_ref[h] for h in hs]
        qs = [_dot((q[h] * qdec_ref[h]).astype(BF16), s[h].astype(BF16)) for h in hs]
        av = [_dot(jnp.concatenate([att[h], (k[h] * kdec_ref[h]).T], axis=0).astype(BF16), vb[h]) for h in hs]
        for h in hs:
            s_scr[h] = s[h] * cdec_ref[h] + av[h][CHUNK:]
            o = av[h][:CHUNK] + qs[h]
            mu = jnp.mean(o, axis=-1, keepdims=True)
            d = o - mu
            var = jnp.mean(d * d, axis=-1, keepdims=True)
            on = d * lax.rsqrt(var + LN_EPS)
            res = _silu(g_ref[rows, cols[h]]) * (on * gnw_ref[:, cols[h]] + gnb_ref[:, cols[h]])
            o_ref[rows, cols[h]] = res.astype(o_ref.dtype)
        return carry

    lax.fori_loop(0, nchunk, chunk, 0)

    @pl.when(tb == pl.num_programs(1) - 1)
    def _():
        sout_ref[0] = s_scr[...]


def _ret_tables(c):
    log_gamma = jnp.log(1.0 - 2.0 ** (-5.0 - jnp.arange(HEADS, dtype=F32)))
    idx = jnp.arange(c, dtype=F32)
    diff = idx[:, None] - idx[None, :]
    causal = diff >= 0
    intra = jnp.where(causal, jnp.exp(log_gamma[:, None, None] * jnp.where(causal, diff, 0.0)), 0.0)
    q_dec = jnp.exp(log_gamma[:, None] * (idx + 1.0))[..., None]
    k_dec = jnp.exp(log_gamma[:, None] * (c - 1.0 - idx))[..., None]
    c_dec = jnp.exp(log_gamma * c)[:, None, None]
    bc = lambda a, r: jnp.broadcast_to(a, (HEADS, r, LANES)).astype(F32)
    return intra, bc(q_dec, c), bc(k_dec, c), bc(c_dec, 1)


def _rope_tables(t, offset):
    half = DH // 2
    inv_freq = ROPE_BASE ** (-jnp.arange(half, dtype=F32) / half)
    pos = jnp.arange(t, dtype=F32) + offset
    ang = pos[:, None] * inv_freq[None, :]
    cos, sin = jnp.cos(ang), jnp.sin(ang)
    return jnp.concatenate([cos, cos], -1), jnp.concatenate([-sin, sin], -1)


def _retention_prompt(p, n, bsz, seq, tt, gnw, gnb):
    nchunk = tt // CHUNK
    nt = seq // tt
    cos2, sin2 = _rope_tables(seq, 0)
    intra, qdec, kdec, cdec = _ret_tables(CHUNK)
    col = lambda c: pl.BlockSpec((tt, HW), lambda b, t, c=c: (b * nt + t, c))
    full = lambda a: pl.BlockSpec(a.shape, lambda b, t: (0,) * a.ndim)
    return pl.pallas_call(
        functools.partial(_ret_body, nchunk=nchunk),
        out_shape=(jax.ShapeDtypeStruct((n, 2 * HW), BF16),
                   jax.ShapeDtypeStruct((bsz, HEADS, DH, DH), F32)),
        grid=(bsz, nt),
        in_specs=[col(COL_QR), col(COL_KR), col(COL_VR), col(COL_GR),
                  pl.BlockSpec(memory_space=pl.ANY),
                  pl.BlockSpec((tt, DH), lambda b, t: (t, 0)),
                  pl.BlockSpec((tt, DH), lambda b, t: (t, 0)),
                  full(intra), full(qdec), full(kdec), full(cdec), full(gnw), full(gnb)],
        out_specs=(pl.BlockSpec((tt, HW), lambda b, t: (b * nt + t, 0)),
                   pl.BlockSpec((1, HEADS, DH, DH), lambda b, t: (b, 0, 0, 0))),
        scratch_shapes=[pltpu.VMEM((HEADS, DH, DH), F32)],
        input_output_aliases={4: 0},
        compiler_params=_cparams(("arbitrary", "arbitrary")),
        name="retention_prompt",
    )(p, p, p, p, jnp.zeros((n, 2 * HW), BF16), cos2, sin2, intra, qdec, kdec, cdec, gnw, gnb)


def _softplus(x):
    return jnp.maximum(x, 0.0) + jnp.log1p(jnp.exp(-jnp.abs(x)))


def _gdn_body(qd_ref, kd_ref, vd_ref, z_ref, tail_ref, o_in_ref, cw_ref, alog_ref, dtb_ref, nw_ref,
              o_ref, sout_ref, ctail_ref, s_scr, prev_scr, xe_scr, qn_scr, kn_scr, vn_scr, g_scr, b_scr,
              *, nchunk, tt):
    del o_in_ref
    tb = pl.program_id(1)

    @pl.when(tb == 0)
    def _():
        s_scr[...] = jnp.zeros_like(s_scr)
        prev_scr[...] = jnp.zeros_like(prev_scr)

    for gi, (src, dst) in enumerate(((qd_ref, qn_scr), (kd_ref, kn_scr), (vd_ref, vn_scr))):
        gcols = slice(gi * HW, (gi + 1) * HW)
        xe_scr[0:SUBLANES, :] = prev_scr[gi]
        xe_scr[SUBLANES:SUBLANES + tt, :] = src[...]
        prev_scr[gi] = src[tt - SUBLANES:tt, :]
        for c in range(nchunk):
            base = SUBLANES + c * CHUNK
            acc = xe_scr[base:base + CHUNK, :] * cw_ref[CONV_W - 1:CONV_W, gcols]
            for j in range(1, CONV_W):
                acc = acc + xe_scr[base - j:base - j + CHUNK, :] * cw_ref[CONV_W - 1 - j:CONV_W - j, gcols]
            y = _silu(acc)
            for h in range(HEADS):
                cs = slice(h * DH, (h + 1) * DH)
                seg = y[:, cs]
                if gi < 2:
                    seg = seg * lax.rsqrt(jnp.sum(seg * seg, axis=-1, keepdims=True) + L2_EPS)
                if gi == 0:
                    seg = seg * (DH ** -0.5)
                dst[c * CHUNK:(c + 1) * CHUNK, cs] = seg

    tail = tail_ref[...]
    g_scr[...] = -jnp.exp(alog_ref[...]) * _softplus(tail + dtb_ref[...])
    b_scr[...] = jax.nn.sigmoid(tail)

    ri = lax.broadcasted_iota(I32, (CHUNK, CHUNK), 0)
    ci_ = lax.broadcasted_iota(I32, (CHUNK, CHUNK), 1)
    causal = ri >= ci_
    strict = ri > ci_
    ltri = jnp.where(causal, 1.0, 0.0).astype(F32)
    same_block = lambda size: (lax.shift_right_logical(ri, size.bit_length() - 1)
                               == lax.shift_right_logical(ci_, size.bit_length() - 1))
    base_mask = same_block(INV_BASE)
    off_masks = []
    size = INV_BASE
    while size < CHUNK:
        off_masks.append(same_block(2 * size) & jnp.logical_not(same_block(size)))
        size *= 2

    def chunk(ci, carry):
        rows = pl.ds(pl.multiple_of(ci * CHUNK, CHUNK), CHUNK)
        gcum = _dot(ltri, g_scr[rows, :], lax.Precision.HIGHEST)
        gcum_t = gcum.T
        bt = b_scr[rows, :]
        hs = range(HEADS)
        cols = [slice(h * DH, (h + 1) * DH) for h in hs]
        gc = [gcum[:, h:h + 1] for h in hs]
        beta = [bt[:, HEADS + h:HEADS + h + 1] for h in hs]
        decay = [jnp.where(causal, jnp.exp(jnp.where(causal, gc[h] - gcum_t[h:h + 1, :], 0.0)), 0.0) for h in hs]
        q = [qn_scr[rows, cols[h]] for h in hs]
        k = [kn_scr[rows, cols[h]] for h in hs]
        kb = [k[h].astype(BF16) for h in hs]
        kq = [_dot_nt(jnp.concatenate([kb[h], q[h].astype(BF16)], axis=0), kb[h]) for h in hs]
        nm = [jnp.where(strict, beta[h] * kq[h][:CHUNK] * decay[h], 0.0) for h in hs]
        pw = [jnp.where(base_mask, nm[h], 0.0) for h in hs]
        e = [-pw[h] for h in hs]
        for _ in range(INV_BASE_LEVELS):
            pwb = [pw[h].astype(BF16) for h in hs]
            pw = [_dot(pwb[h], pwb[h]) for h in hs]
            ep = [_dot(e[h].astype(BF16), pw[h].astype(BF16)) for h in hs]
            e = [e[h] + pw[h] + ep[h] for h in hs]
        for off_mask in off_masks:
            c = [jnp.where(off_mask, nm[h], 0.0) for h in hs]
            x = [c[h] + _dot(e[h].astype(BF16), c[h].astype(BF16)) for h in hs]
            xe = [_dot(x[h].astype(BF16), e[h].astype(BF16)) for h in hs]
            e = [e[h] - (x[h] + xe[h]) for h in hs]
        egc = [jnp.exp(gc[h]) for h in hs]
        rhs = [jnp.concatenate([vn_scr[rows, cols[h]] * beta[h], k[h] * (beta[h] * egc[h])], axis=1) for h in hs]
        uw = [rhs[h] + _dot(e[h].astype(BF16), rhs[h].astype(BF16)) for h in hs]
        s = [s_scr[h] for h in hs]
        sb = [s[h].astype(BF16) for h in hs]
        ws_qs = [_dot(jnp.concatenate([uw[h][:, DH:], q[h] * egc[h]], axis=0).astype(BF16), sb[h])
                 for h in hs]
        vnb = [(uw[h][:, :DH] - ws_qs[h][:CHUNK]).astype(BF16) for h in hs]
        g_last = [gcum[CHUNK - 1:CHUNK, h:h + 1] for h in hs]
        kd_t = [(k[h] * jnp.exp(g_last[h] - gc[h])).T for h in hs]
        av = [_dot(jnp.concatenate([kq[h][CHUNK:] * decay[h], kd_t[h]], axis=0).astype(BF16), vnb[h]) for h in hs]
        for h in hs:
            s_scr[h] = s[h] * jnp.exp(g_last[h]) + av[h][CHUNK:]
            o = ws_qs[h][CHUNK:] + av[h][:CHUNK]
            o = o * lax.rsqrt(jnp.mean(o * o, axis=-1, keepdims=True) + NORM_EPS) * nw_ref[...]
            o = o * _silu(z_ref[rows, cols[h]])
            o_ref[rows, cols[h]] = o.astype(o_ref.dtype)
        return carry

    lax.fori_loop(0, nchunk, chunk, 0)

    @pl.when(tb == pl.num_programs(1) - 1)
    def _():
        sout_ref[0] = s_scr[...]
        ctail_ref[0] = prev_scr[...]


def _gdn_prompt(p, o, bsz, seq, tt, cw, alog_row, dtb_row, nw):
    nchunk = tt // CHUNK
    nt = seq // tt
    n = p.shape[0]
    col = lambda c: pl.BlockSpec((tt, HW), lambda b, t, c=c: (b * nt + t, c))
    full = lambda a: pl.BlockSpec(a.shape, lambda b, t: (0,) * a.ndim)
    return pl.pallas_call(
        functools.partial(_gdn_body, nchunk=nchunk, tt=tt),
        out_shape=(jax.ShapeDtypeStruct((n, 2 * HW), BF16),
                   jax.ShapeDtypeStruct((bsz, HEADS, DH, DH), F32),
                   jax.ShapeDtypeStruct((bsz, 3, SUBLANES, HW), F32)),
        grid=(bsz, nt),
        in_specs=[col(COL_QD), col(COL_KD), col(COL_VD), col(COL_ZD),
                  pl.BlockSpec((tt, LANES), lambda b, t: (b * nt + t, TAIL_COL // LANES)),
                  pl.BlockSpec(memory_space=pl.ANY),
                  full(cw), full(alog_row), full(dtb_row), full(nw)],
        out_specs=(pl.BlockSpec((tt, HW), lambda b, t: (b * nt + t, 1)),
                   pl.BlockSpec((1, HEADS, DH, DH), lambda b, t: (b, 0, 0, 0)),
                   pl.BlockSpec((1, 3, SUBLANES, HW), lambda b, t: (b, 0, 0, 0))),
        scratch_shapes=[pltpu.VMEM((HEADS, DH, DH), F32),
                        pltpu.VMEM((3, SUBLANES, HW), F32),
                        pltpu.VMEM((SUBLANES + tt, HW), F32),
                        pltpu.VMEM((tt, HW), F32), pltpu.VMEM((tt, HW), F32), pltpu.VMEM((tt, HW), F32),
                        pltpu.VMEM((tt, LANES), F32), pltpu.VMEM((tt, LANES), F32)],
        input_output_aliases={5: 0},
        compiler_params=_cparams(("arbitrary", "arbitrary")),
        name="gdn_prompt",
    )(p, p, p, p, p, o, cw, alog_row, dtb_row, nw)


def _to_col(row, eye_mask):
    return jnp.sum(jnp.where(eye_mask, row, 0.0), axis=1, keepdims=True)


def _dec_body(qr_ref, kr_ref, vr_ref, gr_ref, qd_ref, kd_ref, vd_ref, zd_ref, tail_ref,
              sret_ref, sgdn_ref, sconv_ref, o_in_ref, cos_ref, sin_ref, gam_ref, cw_ref, alog_ref, dtb_ref,
              gnw_ref, gnb_ref, nw_ref,
              o_ref, sret_o, sgdn_o, conv_o,
              qr_s, kr_s, qd_s, kd_s, vd_s, gb_s, o_scr, *, bb):
    del o_in_ref
    step = pl.program_id(0)

    @pl.when(step == 0)
    def _():
        cos2 = cos_ref[...]
        sin2 = sin_ref[...]
        for h in range(HEADS):
            cs = slice(h * DH, (h + 1) * DH)
            qr_s[:, cs] = _rope(qr_ref[:, cs], cos2, sin2)
            kr_s[:, cs] = _rope(kr_ref[:, cs], cos2, sin2) * (DH ** -0.5)
        for gi, (src, dst) in enumerate(((qd_ref, qd_s), (kd_ref, kd_s), (vd_ref, vd_s))):
            gcols = slice(gi * HW, (gi + 1) * HW)
            x = src[...]
            acc = x * cw_ref[CONV_W - 1:CONV_W, gcols]
            for i in range(CONV_W - 1):
                acc = acc + sconv_ref[i, :, gcols] * cw_ref[i:i + 1, gcols]
            for i in range(CONV_W - 2):
                conv_o[i, :, gcols] = sconv_ref[i + 1, :, gcols]
            conv_o[CONV_W - 2, :, gcols] = x
            y = _silu(acc)
            for h in range(HEADS):
                cs = slice(h * DH, (h + 1) * DH)
                seg = y[:, cs]
                if gi < 2:
                    seg = seg * lax.rsqrt(jnp.sum(seg * seg, axis=-1, keepdims=True) + L2_EPS)
                if gi == 0:
                    seg = seg * (DH ** -0.5)
                dst[:, cs] = seg
        tail = tail_ref[...]
        lane = lax.broadcasted_iota(I32, tail.shape, 1)
        g = -jnp.exp(alog_ref[...]) * _softplus(tail + dtb_ref[...])
        gb_s[...] = jnp.where(lane < HEADS, g, jax.nn.sigmoid(tail))

    eye_mask = lax.broadcasted_iota(I32, (DH, DH), 0) == lax.broadcasted_iota(I32, (DH, DH), 1)

    rows8 = pl.ds(pl.multiple_of(step * bb, bb), bb)
    sub = lax.broadcasted_iota(I32, (bb, DH), 0)

    def per_seq(j, o8):
        pick = lambda ref, cs: jnp.sum(jnp.where(sub == j, ref[rows8, cs], 0.0), axis=0, keepdims=True)
        gb = pick(gb_s, slice(0, LANES))
        pieces_r, pieces_d = [], []
        for h in range(HEADS):
            cs = slice(h * DH, (h + 1) * DH)
            s = sret_ref[j, h]
            kcol = _to_col(pick(kr_s, cs), eye_mask)
            qcol = _to_col(pick(qr_s, cs), eye_mask)
            sn = s * gam_ref[h] + kcol * pick(vr_ref, cs)
            sret_o[j, h] = sn
            o = jnp.sum(qcol * sn, axis=0, keepdims=True)
            mu = jnp.mean(o, axis=-1, keepdims=True)
            d = o - mu
            var = jnp.mean(d * d, axis=-1, keepdims=True)
            on = d * lax.rsqrt(var + LN_EPS)
            pieces_r.append(_silu(pick(gr_ref, cs)) * (on * gnw_ref[:, cs] + gnb_ref[:, cs]))
            s = sgdn_ref[j, h]
            eg = jnp.exp(gb[:, h:h + 1])
            beta = gb[:, HEADS + h:HEADS + h + 1]
            kcol = _to_col(pick(kd_s, cs), eye_mask)
            qcol = _to_col(pick(qd_s, cs), eye_mask)
            ks = jnp.sum(kcol * s, axis=0, keepdims=True)
            v_new = beta * (pick(vd_s, cs) - eg * ks)
            sn = s * eg + kcol * v_new
            sgdn_o[j, h] = sn
            o = jnp.sum(qcol * sn, axis=0, keepdims=True)
            o = o * lax.rsqrt(jnp.mean(o * o, axis=-1, keepdims=True) + NORM_EPS) * nw_ref[...]
            pieces_d.append(o * _silu(pick(zd_ref, cs)))
        orow = jnp.concatenate(pieces_r + pieces_d, axis=1)
        return jnp.where(lax.broadcasted_iota(I32, o8.shape, 0) == j, orow, o8)

    o_scr[rows8, :] = lax.fori_loop(0, bb, per_seq, jnp.zeros((bb, 2 * HW), F32))

    @pl.when(step == pl.num_programs(0) - 1)
    def _():
        o_ref[...] = o_scr[...].astype(o_ref.dtype)


def _mixer_sample(p, o, n_prompt, dbs, s_ret, s_gdn, s_conv_t, offset, cw, alog_row, dtb_row, gnw, gnb, nw):
    bb = SUBLANES
    n = p.shape[0]
    rb = n_prompt // dbs
    cos2, sin2 = _rope_tables(1, offset)
    log_gamma = jnp.log(1.0 - 2.0 ** (-5.0 - jnp.arange(HEADS, dtype=F32)))
    gam = jnp.broadcast_to(jnp.exp(log_gamma)[:, None, None], (HEADS, 1, LANES)).astype(F32)
    col = lambda c: pl.BlockSpec((dbs, HW), lambda s, c=c: (rb, c))
    full = lambda a: pl.BlockSpec(a.shape, lambda s: (0,) * a.ndim)
    st = pl.BlockSpec((bb, HEADS, DH, DH), lambda s: (s, 0, 0, 0))
    vec = lambda: pltpu.VMEM((dbs, HW), F32)
    return pl.pallas_call(
        functools.partial(_dec_body, bb=bb),
        out_shape=(jax.ShapeDtypeStruct((n, 2 * HW), BF16),
                   jax.ShapeDtypeStruct(s_ret.shape, F32),
                   jax.ShapeDtypeStruct(s_gdn.shape, F32),
                   jax.ShapeDtypeStruct(s_conv_t.shape, F32)),
        grid=(dbs // bb,),
        in_specs=[col(COL_QR), col(COL_KR), col(COL_VR), col(COL_GR),
                  col(COL_QD), col(COL_KD), col(COL_VD), col(COL_ZD),
                  pl.BlockSpec((dbs, LANES), lambda s: (rb, TAIL_COL // LANES)),
                  st, st, full(s_conv_t),
                  pl.BlockSpec(memory_space=pl.ANY),
                  full(cos2), full(sin2), full(gam), full(cw), full(alog_row), full(dtb_row),
                  full(gnw), full(gnb), full(nw)],
        out_specs=(pl.BlockSpec((dbs, 2 * HW), lambda s: (rb, 0)), st, st, full(s_conv_t)),
        scratch_shapes=[vec(), vec(), vec(), vec(), vec(),
                        pltpu.VMEM((dbs, LANES), F32), pltpu.VMEM((dbs, 2 * HW), F32)],
        input_output_aliases={12: 0},
        compiler_params=_cparams(("arbitrary",)),
        name="mixer_sample",
    )(p, p, p, p, p, p, p, p, p, s_ret, s_gdn, s_conv_t, o, cos2, sin2, gam, cw, alog_row, dtb_row,
      gnw, gnb, nw)


def _post_mixer_body(o_ref, x_ref, wout_ref, g_ref, b_ref, wr_ref, br_ref,
                     x1_ref, x1t_ref, e4_ref, r4_ref, g4_ref, cnt_ref, carry_scr, *, alpha, tm):
    i = pl.program_id(0)

    @pl.when(i == 0)
    def _():
        carry_scr[...] = jnp.zeros_like(carry_scr)

    y = _dot(o_ref[...], wout_ref[...])
    x1 = _layer_norm(alpha * x_ref[...] + y, g_ref[...], b_ref[...])
    x1_ref[...] = x1
    _rows_to_tiles(x1t_ref, x1)
    logits = _dot(x1.astype(BF16), wr_ref[...]) + br_ref[...]
    lane = lax.broadcasted_iota(I32, (tm, LANES), 1)
    lane_f = lane.astype(F32)
    work = logits
    ohs, vals, idxs = [], [], []
    for _ in range(TOP_K):
        m = jnp.max(work, axis=1, keepdims=True)
        idx = jnp.min(jnp.where(work == m, lane_f, float(LANES)), axis=1, keepdims=True)
        oh = lane_f == idx
        ohs.append(oh)
        vals.append(m)
        idxs.append(idx)
        work = jnp.where(oh, NEG, work)
    es = [jnp.exp(v - vals[0]) for v in vals]
    den = es[0] + es[1] + es[2] + es[3]
    sel = jnp.zeros((tm, LANES), F32)
    for oh in ohs:
        sel = sel + jnp.where(oh, 1.0, 0.0)
    r_i = lax.broadcasted_iota(I32, (tm, tm), 0)
    c_i = lax.broadcasted_iota(I32, (tm, tm), 1)
    ltri = jnp.where(r_i >= c_i, 1.0, 0.0).astype(BF16)
    incl = _dot(ltri, sel.astype(BF16))
    excl = incl - sel + carry_scr[0:1, :]
    carry_scr[...] = carry_scr[...] + incl[tm - 1:tm, :]
    e4 = jnp.zeros((tm, LANES), F32)
    r4 = jnp.zeros((tm, LANES), F32)
    g4 = jnp.zeros((tm, LANES), F32)
    for k in range(TOP_K):
        rk = jnp.sum(jnp.where(ohs[k], excl, 0.0), axis=1, keepdims=True)
        e4 = jnp.where(lane == k, idxs[k], e4)
        r4 = jnp.where(lane == k, rk, r4)
        g4 = jnp.where(lane == k, es[k] / den, g4)
    e4_ref[...] = e4.astype(I32)
    r4_ref[...] = r4.astype(I32)
    g4_ref[...] = g4
    cnt_ref[...] = carry_scr[...]


def _post_mixer(o, x, wout_bf, ln_g, ln_b, wr_bf, br_row, alpha, tm):
    n, d = x.shape
    row = lambda w: pl.BlockSpec((tm, w), lambda i: (i, 0))
    full = lambda a: pl.BlockSpec(a.shape, lambda i: (0,) * a.ndim)
    return pl.pallas_call(
        functools.partial(_post_mixer_body, alpha=alpha, tm=tm),
        out_shape=(jax.ShapeDtypeStruct((n, d), F32),
                   jax.ShapeDtypeStruct((n * SUBLANES, LANES), F32),
                   jax.ShapeDtypeStruct((n, LANES), I32),
                   jax.ShapeDtypeStruct((n, LANES), I32),
                   jax.ShapeDtypeStruct((n, LANES), F32),
                   jax.ShapeDtypeStruct((SUBLANES, LANES), F32)),
        grid=(n // tm,),
        in_specs=[row(2 * HW), row(d), full(wout_bf), full(ln_g), full(ln_b), full(wr_bf), full(br_row)],
        out_specs=(row(d), pl.BlockSpec((tm * SUBLANES, LANES), lambda i: (i, 0)),
                   row(LANES), row(LANES), row(LANES),
                   pl.BlockSpec((SUBLANES, LANES), lambda i: (0, 0))),
        scratch_shapes=[pltpu.VMEM((SUBLANES, LANES), F32)],
        compiler_params=_cparams(("arbitrary",)),
        name="post_mixer",
    )(o, x, wout_bf, ln_g, ln_b, wr_bf, br_row)


def _plan_body(e4_ref, r4_ref, cnt_ref, d4_ref, blk_ref, seg_ref, *, tm, n_experts, rows):
    nbp = blk_ref.shape[0]
    cnt = cnt_ref[...]
    padded = jnp.floor((cnt + (MOE_BM - 1.0)) * (1.0 / MOE_BM)) * MOE_BM
    m_i = lax.broadcasted_iota(I32, (LANES, LANES), 0)
    j_i = lax.broadcasted_iota(I32, (LANES, LANES), 1)
    upper = jnp.where(m_i <= j_i, 1.0, 0.0).astype(F32)
    pad_end = _dot(padded, upper, lax.Precision.HIGHEST)
    pad_start = (pad_end - padded)[0:1, :]

    lane = lax.broadcasted_iota(I32, (tm, LANES), 1)
    e4 = e4_ref[...]
    r4 = r4_ref[...].astype(F32)
    d4 = jnp.zeros((tm, LANES), F32)
    for k in range(TOP_K):
        ek = e4[:, k:k + 1]
        ps = jnp.sum(jnp.where(lane == ek, pad_start, 0.0), axis=1, keepdims=True)
        d4 = jnp.where(lane == k, ps + r4[:, k:k + 1], d4)
    d4_ref[...] = d4.astype(I32)

    bstart = lax.broadcasted_iota(I32, (nbp, LANES), 0).astype(F32) * MOE_BM
    blane = lax.broadcasted_iota(I32, (nbp, LANES), 1)
    hit = jnp.where((pad_end[0:1, :] <= bstart) & (blane < n_experts), 1.0, 0.0)
    be = jnp.minimum(jnp.sum(hit, axis=1, keepdims=True), n_experts - 1.0)
    blk_ref[...] = jnp.broadcast_to(be, (nbp, LANES)).astype(I32)
    srow = lax.broadcasted_iota(I32, (SUBLANES, LANES), 0)
    slane = lax.broadcasted_iota(I32, (SUBLANES, LANES), 1)
    total = jnp.sum(jnp.where(slane == n_experts - 1, pad_end, 0.0), axis=1, keepdims=True)
    seg_end = jnp.where(slane == n_experts - 1, float(rows), pad_end)
    seg = jnp.where(srow == 0, pad_end - padded + cnt, jnp.where(srow == 1, seg_end, total * (1.0 / MOE_BM)))
    seg_ref[...] = seg.astype(I32)


def _plan(e4, r4, cnt, tm, n_experts, rows):
    n = e4.shape[0]
    nbp = pl.cdiv(rows // MOE_BM, SUBLANES) * SUBLANES
    row = pl.BlockSpec((tm, LANES), lambda i: (i, 0))
    return pl.pallas_call(
        functools.partial(_plan_body, tm=tm, n_experts=n_experts, rows=rows),
        out_shape=(jax.ShapeDtypeStruct((n, LANES), I32),
                   jax.ShapeDtypeStruct((nbp, LANES), I32),
                   jax.ShapeDtypeStruct((SUBLANES, LANES), I32)),
        grid=(n // tm,),
        in_specs=[row, row, pl.BlockSpec((SUBLANES, LANES), lambda i: (0, 0))],
        out_specs=(row, pl.BlockSpec((nbp, LANES), lambda i: (0, 0)),
                   pl.BlockSpec((SUBLANES, LANES), lambda i: (0, 0))),
        compiler_params=_cparams(("arbitrary",)),
        name="moe_plan",
    )(e4, r4, cnt)


def _tile_at(ref, r, lead=()):
    return ref.at[lead + (pl.ds(pl.multiple_of(r * SUBLANES, SUBLANES), SUBLANES), slice(None))]


def _issue_and_wait(copy_of, count):
    def start(i, carry):
        for u in range(DMA_UNROLL):
            copy_of(i * DMA_UNROLL + u).start(priority=u % 2)
        return carry

    def wait(pidx, carry):
        copy_of(pidx).wait()
        return carry

    lax.fori_loop(0, count // DMA_UNROLL, start, 0)
    lax.fori_loop(0, count, wait, 0, unroll=DMA_UNROLL)


def _dispatch_body(zs_ref, ze_ref, dest_ref, x_ref, xs_ref, zero_scr, sem, zsem, *, n_experts):
    def token_copy(pidx):
        t = lax.shift_right_logical(pidx, 2)
        return pltpu.make_async_copy(_tile_at(x_ref, t), _tile_at(xs_ref, dest_ref[0, 0, pidx]), sem)

    _issue_and_wait(token_copy, TOK_TILE * TOP_K)

    @pl.when(pl.program_id(0) == pl.num_programs(0) - 1)
    def _():
        zero_scr[...] = jnp.zeros_like(zero_scr)

        def zero_copy(r):
            return pltpu.make_async_copy(zero_scr, _tile_at(xs_ref, r), zsem)

        def per_expert(wait):
            def body(e, carry):
                def one(r, c):
                    zero_copy(r).wait() if wait else zero_copy(r).start()
                    return c
                lax.fori_loop(zs_ref[e], ze_ref[e], one, 0)
                return carry
            lax.fori_loop(0, n_experts, body, 0)

        per_expert(False)
        per_expert(True)


def _dispatch(x1t, dest3, zstart, zend, rows, n_experts):
    n = x1t.shape[0] // SUBLANES
    npairs = TOK_TILE * TOP_K
    grid_spec = pltpu.PrefetchScalarGridSpec(
        num_scalar_prefetch=2,
        grid=(n // TOK_TILE,),
        in_specs=[pl.BlockSpec((1, 1, npairs), lambda i, zs, ze: (i, 0, 0), memory_space=pltpu.SMEM),
                  pl.BlockSpec((TOK_TILE * SUBLANES, LANES), lambda i, zs, ze: (i, 0))],
        out_specs=pl.BlockSpec(memory_space=pl.ANY),
        scratch_shapes=[pltpu.VMEM((SUBLANES, LANES), F32), pltpu.SemaphoreType.DMA(()),
                        pltpu.SemaphoreType.DMA(())],
    )
    return pl.pallas_call(
        functools.partial(_dispatch_body, n_experts=n_experts),
        out_shape=jax.ShapeDtypeStruct((rows * SUBLANES, LANES), F32),
        grid_spec=grid_spec,
        compiler_params=_cparams(("arbitrary",)),
        name="moe_dispatch",
    )(zstart, zend, dest3, x1t)


def _ffn_body(blk_ref, nused_ref, xs_ref, wgu_ref, bgu_ref, wd_ref, bd_ref, ys_ref, wgu_bf, wd_bf, *, de):
    i = pl.program_id(0)
    e = blk_ref[i]
    prev = blk_ref[jnp.maximum(i - 1, 0)]

    @pl.when((i == 0) | (e != prev))
    def _():
        rb = 128
        for r in range(0, wgu_ref.shape[0], rb):
            wgu_bf[r:r + rb, :] = wgu_ref[r:r + rb, :].astype(BF16)
        for r in range(0, wd_ref.shape[0], rb):
            wd_bf[r:r + rb, :] = wd_ref[r:r + rb, :].astype(BF16)

    @pl.when(i < nused_ref[0])
    def _():
        xb = _tiles_to_rows(xs_ref, MOE_BM).astype(BF16)
        gu = _dot(xb, wgu_bf[...]) + bgu_ref[...]
        gate = jnp.minimum(gu[:, :de], SWIGLU_LIMIT)
        up = jnp.clip(gu[:, de:], -SWIGLU_LIMIT, SWIGLU_LIMIT)
        act = (up + 1.0) * gate * jax.nn.sigmoid(SWIGLU_ALPHA * gate)
        _rows_to_tiles(ys_ref, _dot(act.astype(BF16), wd_bf[...]) + bd_ref[...])

    @pl.when(i >= nused_ref[0])
    def _():
        ys_ref[...] = jnp.zeros_like(ys_ref)


def _ffn(xs, blk_e, nused, w_gu, b_gu, w_down, b_down):
    rows = xs.shape[0] // SUBLANES
    n_experts, d, de2 = w_gu.shape
    de = de2 // 2
    nb = rows // MOE_BM
    tile_blk = pl.BlockSpec((MOE_BM * SUBLANES, LANES), lambda i, blk, nu: (i, 0))
    grid_spec = pltpu.PrefetchScalarGridSpec(
        num_scalar_prefetch=2,
        grid=(nb,),
        in_specs=[tile_blk,
                  pl.BlockSpec((None, d, de2), lambda i, blk, nu: (blk[i], 0, 0)),
                  pl.BlockSpec((None, 1, de2), lambda i, blk, nu: (blk[i], 0, 0)),
                  pl.BlockSpec((None, de, d), lambda i, blk, nu: (blk[i], 0, 0)),
                  pl.BlockSpec((None, 1, d), lambda i, blk, nu: (blk[i], 0, 0))],
        out_specs=tile_blk,
        scratch_shapes=[pltpu.VMEM((d, de2), BF16), pltpu.VMEM((de, d), BF16)],
    )
    return pl.pallas_call(
        functools.partial(_ffn_body, de=de),
        out_shape=jax.ShapeDtypeStruct((rows * SUBLANES, LANES), F32),
        grid_spec=grid_spec,
        compiler_params=_cparams(("arbitrary",)),
        name="moe_ffn",
    )(blk_e, nused, xs, w_gu, b_gu.reshape(n_experts, 1, de2), w_down, b_down.reshape(n_experts, 1, d))


def _combine_body(dest_ref, g4_ref, x1_ref, g_ref, b_ref, ys_ref, *rest, alpha, split_tiles):
    *out_refs, buf, sem = rest

    def row_copy(pidx):
        t = lax.shift_right_logical(pidx, 2)
        k = lax.bitwise_and(pidx, TOP_K - 1)
        return pltpu.make_async_copy(_tile_at(ys_ref, dest_ref[0, 0, pidx]), _tile_at(buf, t, (k,)), sem)

    _issue_and_wait(row_copy, TOK_TILE * TOP_K)
    g4 = g4_ref[...]
    f = g4[:, 0:1] * _tiles_to_rows(buf, TOK_TILE, (0,))
    for k in range(1, TOP_K):
        f = f + g4[:, k:k + 1] * _tiles_to_rows(buf, TOK_TILE, (k,))
    res = _layer_norm(alpha * x1_ref[...] + f, g_ref[...], b_ref[...])
    if split_tiles is None:
        out_refs[0][...] = res
    else:
        i = pl.program_id(0)

        @pl.when(i < split_tiles)
        def _():
            out_refs[0][...] = res

        @pl.when(i >= split_tiles)
        def _():
            out_refs[1][...] = res


def _combine(ys, dest3, g4, x1, ln_g, ln_b, alpha, n_prompt=None):
    n, d = x1.shape
    full = lambda a: pl.BlockSpec(a.shape, lambda i: (0,) * a.ndim)
    tile = lambda f: pl.BlockSpec((TOK_TILE, d), f)
    if n_prompt is None:
        split_tiles = None
        out_shape = jax.ShapeDtypeStruct((n, d), F32)
        out_specs = tile(lambda i: (i, 0))
    else:
        assert n - n_prompt == TOK_TILE and n_prompt % TOK_TILE == 0
        split_tiles = n_prompt // TOK_TILE
        out_shape = (jax.ShapeDtypeStruct((n_prompt, d), F32), jax.ShapeDtypeStruct((TOK_TILE, d), F32))
        out_specs = (tile(lambda i: (jnp.minimum(i, split_tiles - 1), 0)), tile(lambda i: (0, 0)))
    return pl.pallas_call(
        functools.partial(_combine_body, alpha=alpha, split_tiles=split_tiles),
        out_shape=out_shape,
        grid=(n // TOK_TILE,),
        in_specs=[pl.BlockSpec((1, 1, TOK_TILE * TOP_K), lambda i: (i, 0, 0), memory_space=pltpu.SMEM),
                  pl.BlockSpec((TOK_TILE, LANES), lambda i: (i, 0)),
                  tile(lambda i: (i, 0)),
                  full(ln_g), full(ln_b),
                  pl.BlockSpec(memory_space=pl.ANY)],
        out_specs=out_specs,
        scratch_shapes=[pltpu.VMEM((TOP_K, TOK_TILE * SUBLANES, LANES), F32), pltpu.SemaphoreType.DMA(())],
        compiler_params=_cparams(("arbitrary",)),
        name="moe_combine",
    )(dest3, g4, x1, ln_g, ln_b, ys)


def kernel(x_prompt, x_sample, state_ret, state_gdn, state_conv, w_in, conv_w, a_log, dt_bias, ret_gn_w, ret_gn_b, gdn_norm_w, w_out, ln1_g, ln1_b, w_router, b_router, w_gu, b_gu, w_down, b_down, ln2_g, ln2_b):
    bsz, seq, d = x_prompt.shape
    dbs, dseq, _ = x_sample.shape
    depth = w_in.shape[0]
    n_experts = w_router.shape[-1]
    assert dseq == 1 and dbs == LANES and seq % CHUNK == 0 and d == 2 * HW == SUBLANES * LANES
    assert 2 ** (INV_BASE_LEVELS + 1) == INV_BASE and CHUNK % INV_BASE == 0
    n_prompt = bsz * seq
    n = n_prompt + dbs
    alpha = (2.0 * depth) ** 0.25
    tm = _pick_tile(n, (384, 256, 128))
    tt = _pick_tile(seq, (512, 256, 128))
    rows = (pl.cdiv(n * TOP_K, MOE_BM) + n_experts) * MOE_BM

    x = jnp.concatenate([x_prompt.reshape(n_prompt, d), x_sample.reshape(dbs, d)], axis=0)
    row2 = lambda a: a.reshape(1, -1).astype(F32)
    pad_lanes = lambda a, fill: jnp.concatenate(
        [a.astype(F32), jnp.full((LANES - a.shape[0],), fill, F32)]).reshape(1, LANES)

    ret_p, gdn_p, conv_p, ret_s, gdn_s, conv_s = [], [], [], [], [], []
    for l in range(depth):
        w_in_bf = jnp.pad(w_in[l], ((0, 0), (0, IN_PAD - w_in.shape[-1]))).astype(BF16)
        p = _inproj(x, w_in_bf, tm)
        gnw, gnb, nw = row2(ret_gn_w[l]), row2(ret_gn_b[l]), row2(gdn_norm_w[l])
        alog_row, dtb_row = pad_lanes(a_log[l], 0.0), pad_lanes(dt_bias[l], 0.0)
        cw = conv_w[l].astype(F32)

        o, sr = _retention_prompt(p, n, bsz, seq, tt, gnw, gnb)
        o, sg, ctail = _gdn_prompt(p, o, bsz, seq, tt, cw, alog_row, dtb_row, nw)
        sconv_t = jnp.transpose(state_conv[l], (1, 0, 2))
        o, sr_s, sg_s, sc_s = _mixer_sample(p, o, n_prompt, dbs, state_ret[l], state_gdn[l], sconv_t,
                                            float(PAST_LEN), cw, alog_row, dtb_row, gnw, gnb, nw)
        ret_p.append(sr)
        gdn_p.append(sg)
        conv_p.append(jnp.transpose(ctail[:, :, SUBLANES - (CONV_W - 1):, :], (0, 2, 1, 3))
                      .reshape(bsz, CONV_W - 1, 3 * HW))
        ret_s.append(sr_s)
        gdn_s.append(sg_s)
        conv_s.append(jnp.transpose(sc_s, (1, 0, 2)))

        wr_bf = jnp.pad(w_router[l], ((0, 0), (0, LANES - n_experts))).astype(BF16)
        br_row = pad_lanes(b_router[l], NEG)
        x1, x1t, e4, r4, g4, cnt = _post_mixer(o, x, w_out[l].astype(BF16), row2(ln1_g[l]), row2(ln1_b[l]),
                                               wr_bf, br_row, alpha, tm)
        d4, blk, seg = _plan(e4, r4, cnt, tm, n_experts, rows)
        dest3 = d4[:, :TOP_K].reshape(n // TOK_TILE, 1, TOK_TILE * TOP_K)
        xs = _dispatch(x1t, dest3, seg[0, :n_experts], seg[1, :n_experts], rows, n_experts)
        ys = _ffn(xs, blk[:rows // MOE_BM, 0], seg[2, :1], w_gu[l], b_gu[l], w_down[l], b_down[l])
        if l + 1 < depth:
            x = _combine(ys, dest3, g4, x1, row2(ln2_g[l]), row2(ln2_b[l]), alpha)
        else:
            y_prompt, y_sample = _combine(ys, dest3, g4, x1, row2(ln2_g[l]), row2(ln2_b[l]), alpha, n_prompt)

    return (y_prompt.reshape(bsz, seq, d), y_sample.reshape(dbs, dseq, d),
            jnp.stack(ret_p), jnp.stack(gdn_p), jnp.stack(conv_p),
            jnp.stack(ret_s), jnp.stack(gdn_s), jnp.stack(conv_s))
```

```python
import functools

import jax
import jax.numpy as jnp
from jax import lax
from jax.experimental import pallas as pl
from jax.experimental.pallas import tpu as pltpu

F32 = jnp.float32
BF16 = jnp.bfloat16
I32 = jnp.int32

HEADS = 4
DH = 128
HW = HEADS * DH
CONV_W = 4
ROPE_BASE = 10000.0
PAST_LEN = 16384
TOP_K = 4
SWIGLU_LIMIT = 7.0
SWIGLU_ALPHA = 1.702
LN_EPS = 1e-5
NORM_EPS = 1e-6
L2_EPS = 1e-6

LANES = 128
SUBLANES = 8
VMEM_LIMIT_BYTES = 56 * 1024 * 1024

CHUNK = 128
INV_BASE = 16
INV_BASE_LEVELS = 3
MOE_BM = 256
TOK_TILE = 128
DMA_UNROLL = 8
NEG = -3.0e38

COL_QR, COL_KR, COL_VR, COL_GR, COL_QD, COL_KD, COL_VD, COL_ZD = range(8)
TAIL_COL = 8 * HW
IN_PAD = TAIL_COL + LANES

_NT = (((1,), (1,)), ((), ()))


def _cparams(sem, vmem=VMEM_LIMIT_BYTES):
    return pltpu.CompilerParams(dimension_semantics=sem, vmem_limit_bytes=vmem)


def _dot(a, b, precision=None):
    return jnp.dot(a, b, preferred_element_type=F32, precision=precision)


def _dot_nt(a, b):
    return lax.dot_general(a, b, _NT, preferred_element_type=F32)


def _silu(x):
    return x * jax.nn.sigmoid(x)


def _layer_norm(h, g, b):
    mu = jnp.mean(h, axis=-1, keepdims=True)
    d = h - mu
    var = jnp.mean(d * d, axis=-1, keepdims=True)
    return d * lax.rsqrt(var + LN_EPS) * g + b


def _rope(x, cos2, sin2):
    return x * cos2 + pltpu.roll(x, DH // 2, 1) * sin2


def _tiles_to_rows(ref, nrows, lead=()):
    return jnp.concatenate([ref[lead + (pl.ds(s, nrows, stride=SUBLANES), slice(None))] for s in range(SUBLANES)],
                           axis=1)


def _rows_to_tiles(ref, val):
    nrows = val.shape[0]
    for s in range(SUBLANES):
        ref[pl.ds(s, nrows, stride=SUBLANES), :] = val[:, s * LANES:(s + 1) * LANES]


def _pick_tile(n, candidates):
    for c in candidates:
        if n % c == 0:
            return c
    raise ValueError(f"no tile in {candidates} divides {n}")


def _inproj_body(x_ref, w_ref, o_ref):
    o_ref[...] = _dot(x_ref[...].astype(BF16), w_ref[...])


def _inproj(x, w_bf, tm):
    n, d = x.shape
    return pl.pallas_call(
        _inproj_body,
        out_shape=jax.ShapeDtypeStruct((n, IN_PAD), F32),
        grid=(n // tm,),
        in_specs=[pl.BlockSpec((tm, d), lambda i: (i, 0)),
                  pl.BlockSpec((d, IN_PAD), lambda i: (0, 0))],
        out_specs=pl.BlockSpec((tm, IN_PAD), lambda i: (i, 0)),
        compiler_params=_cparams(("arbitrary",)),
        name="inproj",
    )(x, w_bf)


def _ret_body(q_ref, k_ref, v_ref, g_ref, o_in_ref, cos_ref, sin_ref, intra_ref, qdec_ref, kdec_ref, cdec_ref,
              gnw_ref, gnb_ref, o_ref, sout_ref, s_scr, *, nchunk):
    del o_in_ref
    tb = pl.program_id(1)

    @pl.when(tb == 0)
    def _():
        s_scr[...] = jnp.zeros_like(s_scr)

    def chunk(ci, carry):
        rows = pl.ds(pl.multiple_of(ci * CHUNK, CHUNK), CHUNK)
        cos2 = cos_ref[rows, :]
        sin2 = sin_ref[rows, :]
        hs = range(HEADS)
        cols = [slice(h * DH, (h + 1) * DH) for h in hs]
        q = [_rope(q_ref[rows, cols[h]], cos2, sin2) for h in hs]
        k = [_rope(k_ref[rows, cols[h]], cos2, sin2) * (DH ** -0.5) for h in hs]
        vb = [v_ref[rows, cols[h]].astype(BF16) for h in hs]
        s = [s_scr[h] for h in hs]
        att = [_dot_nt(q[h].astype(BF16), k[h].astype(BF16)) * intra_ref[h] for h in hs]
        qs = [_dot((q[h] * qdec_ref[h]).astype(BF16), s[h].astype(BF16)) for h in hs]
        av = [_dot(jnp.concatenate([att[h], (k[h] * kdec_ref[h]).T], axis=0).astype(BF16), vb[h]) for h in hs]
        for h in hs:
            s_scr[h] = s[h] * cdec_ref[h] + av[h][CHUNK:]
            o = av[h][:CHUNK] + qs[h]
            mu = jnp.mean(o, axis=-1, keepdims=True)
            d = o - mu
            var = jnp.mean(d * d, axis=-1, keepdims=True)
            on = d * lax.rsqrt(var + LN_EPS)
            res = _silu(g_ref[rows, cols[h]]) * (on * gnw_ref[:, cols[h]] + gnb_ref[:, cols[h]])
            o_ref[rows, cols[h]] = res.astype(o_ref.dtype)
        return carry

    lax.fori_loop(0, nchunk, chunk, 0)

    @pl.when(tb == pl.num_programs(1) - 1)
    def _():
        sout_ref[0] = s_scr[...]


def _ret_tables(c):
    log_gamma = jnp.log(1.0 - 2.0 ** (-5.0 - jnp.arange(HEADS, dtype=F32)))
    idx = jnp.arange(c, dtype=F32)
    diff = idx[:, None] - idx[None, :]
    causal = diff >= 0
    intra = jnp.where(causal, jnp.exp(log_gamma[:, None, None] * jnp.where(causal, diff, 0.0)), 0.0)
    q_dec = jnp.exp(log_gamma[:, None] * (idx + 1.0))[..., None]
    k_dec = jnp.exp(log_gamma[:, None] * (c - 1.0 - idx))[..., None]
    c_dec = jnp.exp(log_gamma * c)[:, None, None]
    bc = lambda a, r: jnp.broadcast_to(a, (HEADS, r, LANES)).astype(F32)
    return intra, bc(q_dec, c), bc(k_dec, c), bc(c_dec, 1)


def _rope_tables(t, offset):
    half = DH // 2
    inv_freq = ROPE_BASE ** (-jnp.arange(half, dtype=F32) / half)
    pos = jnp.arange(t, dtype=F32) + offset
    ang = pos[:, None] * inv_freq[None, :]
    cos, sin = jnp.cos(ang), jnp.sin(ang)
    return jnp.concatenate([cos, cos], -1), jnp.concatenate([-sin, sin], -1)


def _retention_prompt(p, n, bsz, seq, tt, gnw, gnb):
    nchunk = tt // CHUNK
    nt = seq // tt
    cos2, sin2 = _rope_tables(seq, 0)
    intra, qdec, kdec, cdec = _ret_tables(CHUNK)
    col = lambda c: pl.BlockSpec((tt, HW), lambda b, t, c=c: (b * nt + t, c))
    full = lambda a: pl.BlockSpec(a.shape, lambda b, t: (0,) * a.ndim)
    return pl.pallas_call(
        functools.partial(_ret_body, nchunk=nchunk),
        out_shape=(jax.ShapeDtypeStruct((n, 2 * HW), BF16),
                   jax.ShapeDtypeStruct((bsz, HEADS, DH, DH), F32)),
        grid=(bsz, nt),
        in_specs=[col(COL_QR), col(COL_KR), col(COL_VR), col(COL_GR),
                  pl.BlockSpec(memory_space=pl.ANY),
                  pl.BlockSpec((tt, DH), lambda b, t: (t, 0)),
                  pl.BlockSpec((tt, DH), lambda b, t: (t, 0)),
                  full(intra), full(qdec), full(kdec), full(cdec), full(gnw), full(gnb)],
        out_specs=(pl.BlockSpec((tt, HW), lambda b, t: (b * nt + t, 0)),
                   pl.BlockSpec((1, HEADS, DH, DH), lambda b, t: (b, 0, 0, 0))),
        scratch_shapes=[pltpu.VMEM((HEADS, DH, DH), F32)],
        input_output_aliases={4: 0},
        compiler_params=_cparams(("arbitrary", "arbitrary")),
        name="retention_prompt",
    )(p, p, p, p, jnp.zeros((n, 2 * HW), BF16), cos2, sin2, intra, qdec, kdec, cdec, gnw, gnb)


def _softplus(x):
    return jnp.maximum(x, 0.0) + jnp.log1p(jnp.exp(-jnp.abs(x)))


def _gdn_body(qd_ref, kd_ref, vd_ref, z_ref, tail_ref, o_in_ref, cw_ref, alog_ref, dtb_ref, nw_ref,
              o_ref, sout_ref, ctail_ref, s_scr, prev_scr, xe_scr, qn_scr, kn_scr, vn_scr, g_scr, b_scr,
              *, nchunk, tt):
    del o_in_ref
    tb = pl.program_id(1)

    @pl.when(tb == 0)
    def _():
        s_scr[...] = jnp.zeros_like(s_scr)
        prev_scr[...] = jnp.zeros_like(prev_scr)

    for gi, (src, dst) in enumerate(((qd_ref, qn_scr), (kd_ref, kn_scr), (vd_ref, vn_scr))):
        gcols = slice(gi * HW, (gi + 1) * HW)
        xe_scr[0:SUBLANES, :] = prev_scr[gi]
        xe_scr[SUBLANES:SUBLANES + tt, :] = src[...]
        prev_scr[gi] = src[tt - SUBLANES:tt, :]
        for c in range(nchunk):
            base = SUBLANES + c * CHUNK
            acc = xe_scr[base:base + CHUNK, :] * cw_ref[CONV_W - 1:CONV_W, gcols]
            for j in range(1, CONV_W):
                acc = acc + xe_scr[base - j:base - j + CHUNK, :] * cw_ref[CONV_W - 1 - j:CONV_W - j, gcols]
            y = _silu(acc)
            for h in range(HEADS):
                cs = slice(h * DH, (h + 1) * DH)
                seg = y[:, cs]
                if gi < 2:
                    seg = seg * lax.rsqrt(jnp.sum(seg * seg, axis=-1, keepdims=True) + L2_EPS)
                if gi == 0:
                    seg = seg * (DH ** -0.5)
                dst[c * CHUNK:(c + 1) * CHUNK, cs] = seg

    tail = tail_ref[...]
    g_scr[...] = -jnp.exp(alog_ref[...]) * _softplus(tail + dtb_ref[...])
    b_scr[...] = jax.nn.sigmoid(tail)

    ri = lax.broadcasted_iota(I32, (CHUNK, CHUNK), 0)
    ci_ = lax.broadcasted_iota(I32, (CHUNK, CHUNK), 1)
    causal = ri >= ci_
    strict = ri > ci_
    ltri = jnp.where(causal, 1.0, 0.0).astype(F32)
    same_block = lambda size: (lax.shift_right_logical(ri, size.bit_length() - 1)
                               == lax.shift_right_logical(ci_, size.bit_length() - 1))
    base_mask = same_block(INV_BASE)
    off_masks = []
    size = INV_BASE
    while size < CHUNK:
        off_masks.append(same_block(2 * size) & jnp.logical_not(same_block(size)))
        size *= 2

    def chunk(ci, carry):
        rows = pl.ds(pl.multiple_of(ci * CHUNK, CHUNK), CHUNK)
        gcum = _dot(ltri, g_scr[rows, :], lax.Precision.HIGHEST)
        gcum_t = gcum.T
        bt = b_scr[rows, :]
        hs = range(HEADS)
        cols = [slice(h * DH, (h + 1) * DH) for h in hs]
        gc = [gcum[:, h:h + 1] for h in hs]
        beta = [bt[:, HEADS + h:HEADS + h + 1] for h in hs]
        decay = [jnp.where(causal, jnp.exp(jnp.where(causal, gc[h] - gcum_t[h:h + 1, :], 0.0)), 0.0) for h in hs]
        q = [qn_scr[rows, cols[h]] for h in hs]
        k = [kn_scr[rows, cols[h]] for h in hs]
        kb = [k[h].astype(BF16) for h in hs]
        kq = [_dot_nt(jnp.concatenate([kb[h], q[h].astype(BF16)], axis=0), kb[h]) for h in hs]
        nm = [jnp.where(strict, beta[h] * kq[h][:CHUNK] * decay[h], 0.0) for h in hs]
        pw = [jnp.where(base_mask, nm[h], 0.0) for h in hs]
        e = [-pw[h] for h in hs]
        for _ in range(INV_BASE_LEVELS):
            pwb = [pw[h].astype(BF16) for h in hs]
            pw = [_dot(pwb[h], pwb[h]) for h in hs]
            ep = [_dot(e[h].astype(BF16), pw[h].astype(BF16)) for h in hs]
            e = [e[h] + pw[h] + ep[h] for h in hs]
        for off_mask in off_masks:
            c = [jnp.where(off_mask, nm[h], 0.0) for h in hs]
            x = [c[h] + _dot(e[h].astype(BF16), c[h].astype(BF16)) for h in hs]
            xe = [_dot(x[h].astype(BF16), e[h].astype(BF16)) for h in hs]
            e = [e[h] - (x[h] + xe[h]) for h in hs]
        egc = [jnp.exp(gc[h]) for h in hs]
        rhs = [jnp.concatenate([vn_scr[rows, cols[h]] * beta[h], k[h] * (beta[h] * egc[h])], axis=1) for h in hs]
        uw = [rhs[h] + _dot(e[h].astype(BF16), rhs[h].astype(BF16)) for h in hs]
        s = [s_scr[h] for h in hs]
        sb = [s[h].astype(BF16) for h in hs]
        ws_qs = [_dot(jnp.concatenate([uw[h][:, DH:], q[h] * egc[h]], axis=0).astype(BF16), sb[h])
                 for h in hs]
        vnb = [(uw[h][:, :DH] - ws_qs[h][:CHUNK]).astype(BF16) for h in hs]
        g_last = [gcum[CHUNK - 1:CHUNK, h:h + 1] for h in hs]
        kd_t = [(k[h] * jnp.exp(g_last[h] - gc[h])).T for h in hs]
        av = [_dot(jnp.concatenate([kq[h][CHUNK:] * decay[h], kd_t[h]], axis=0).astype(BF16), vnb[h]) for h in hs]
        for h in hs:
            s_scr[h] = s[h] * jnp.exp(g_last[h]) + av[h][CHUNK:]
            o = ws_qs[h][CHUNK:] + av[h][:CHUNK]
            o = o * lax.rsqrt(jnp.mean(o * o, axis=-1, keepdims=True) + NORM_EPS) * nw_ref[...]
            o = o * _silu(z_ref[rows, cols[h]])
            o_ref[rows, cols[h]] = o.astype(o_ref.dtype)
        return carry

    lax.fori_loop(0, nchunk, chunk, 0)

    @pl.when(tb == pl.num_programs(1) - 1)
    def _():
        sout_ref[0] = s_scr[...]
        ctail_ref[0] = prev_scr[...]


def _gdn_prompt(p, o, bsz, seq, tt, cw, alog_row, dtb_row, nw):
    nchunk = tt // CHUNK
    nt = seq // tt
    n = p.shape[0]
    col = lambda c: pl.BlockSpec((tt, HW), lambda b, t, c=c: (b * nt + t, c))
    full = lambda a: pl.BlockSpec(a.shape, lambda b, t: (0,) * a.ndim)
    return pl.pallas_call(
        functools.partial(_gdn_body, nchunk=nchunk, tt=tt),
        out_shape=(jax.ShapeDtypeStruct((n, 2 * HW), BF16),
                   jax.ShapeDtypeStruct((bsz, HEADS, DH, DH), F32),
                   jax.ShapeDtypeStruct((bsz, 3, SUBLANES, HW), F32)),
        grid=(bsz, nt),
        in_specs=[col(COL_QD), col(COL_KD), col(COL_VD), col(COL_ZD),
                  pl.BlockSpec((tt, LANES), lambda b, t: (b * nt + t, TAIL_COL // LANES)),
                  pl.BlockSpec(memory_space=pl.ANY),
                  full(cw), full(alog_row), full(dtb_row), full(nw)],
        out_specs=(pl.BlockSpec((tt, HW), lambda b, t: (b * nt + t, 1)),
                   pl.BlockSpec((1, HEADS, DH, DH), lambda b, t: (b, 0, 0, 0)),
                   pl.BlockSpec((1, 3, SUBLANES, HW), lambda b, t: (b, 0, 0, 0))),
        scratch_shapes=[pltpu.VMEM((HEADS, DH, DH), F32),
                        pltpu.VMEM((3, SUBLANES, HW), F32),
                        pltpu.VMEM((SUBLANES + tt, HW), F32),
                        pltpu.VMEM((tt, HW), F32), pltpu.VMEM((tt, HW), F32), pltpu.VMEM((tt, HW), F32),
                        pltpu.VMEM((tt, LANES), F32), pltpu.VMEM((tt, LANES), F32)],
        input_output_aliases={5: 0},
        compiler_params=_cparams(("arbitrary", "arbitrary")),
        name="gdn_prompt",
    )(p, p, p, p, p, o, cw, alog_row, dtb_row, nw)


def _to_col(row, eye_mask):
    return jnp.sum(jnp.where(eye_mask, row, 0.0), axis=1, keepdims=True)


def _dec_body(qr_ref, kr_ref, vr_ref, gr_ref, qd_ref, kd_ref, vd_ref, zd_ref, tail_ref,
              sret_ref, sgdn_ref, sconv_ref, o_in_ref, cos_ref, sin_ref, gam_ref, cw_ref, alog_ref, dtb_ref,
              gnw_ref, gnb_ref, nw_ref,
              o_ref, sret_o, sgdn_o, conv_o,
              qr_s, kr_s, qd_s, kd_s, vd_s, gb_s, o_scr, *, bb):
    del o_in_ref
    step = pl.program_id(0)

    @pl.when(step == 0)
    def _():
        cos2 = cos_ref[...]
        sin2 = sin_ref[...]
        for h in range(HEADS):
            cs = slice(h * DH, (h + 1) * DH)
            qr_s[:, cs] = _rope(qr_ref[:, cs], cos2, sin2)
            kr_s[:, cs] = _rope(kr_ref[:, cs], cos2, sin2) * (DH ** -0.5)
        for gi, (src, dst) in enumerate(((qd_ref, qd_s), (kd_ref, kd_s), (vd_ref, vd_s))):
            gcols = slice(gi * HW, (gi + 1) * HW)
            x = src[...]
            acc = x * cw_ref[CONV_W - 1:CONV_W, gcols]
            for i in range(CONV_W - 1):
                acc = acc + sconv_ref[i, :, gcols] * cw_ref[i:i + 1, gcols]
            for i in range(CONV_W - 2):
                conv_o[i, :, gcols] = sconv_ref[i + 1, :, gcols]
            conv_o[CONV_W - 2, :, gcols] = x
            y = _silu(acc)
            for h in range(HEADS):
                cs = slice(h * DH, (h + 1) * DH)
                seg = y[:, cs]
                if gi < 2:
                    seg = seg * lax.rsqrt(jnp.sum(seg * seg, axis=-1, keepdims=True) + L2_EPS)
                if gi == 0:
                    seg = seg * (DH ** -0.5)
                dst[:, cs] = seg
        tail = tail_ref[...]
        lane = lax.broadcasted_iota(I32, tail.shape, 1)
        g = -jnp.exp(alog_ref[...]) * _softplus(tail + dtb_ref[...])
        gb_s[...] = jnp.where(lane < HEADS, g, jax.nn.sigmoid(tail))

    eye_mask = lax.broadcasted_iota(I32, (DH, DH), 0) == lax.broadcasted_iota(I32, (DH, DH), 1)

    rows8 = pl.ds(pl.multiple_of(step * bb, bb), bb)
    sub = lax.broadcasted_iota(I32, (bb, DH), 0)

    def per_seq(j, o8):
        pick = lambda ref, cs: jnp.sum(jnp.where(sub == j, ref[rows8, cs], 0.0), axis=0, keepdims=True)
        gb = pick(gb_s, slice(0, LANES))
        pieces_r, pieces_d = [], []
        for h in range(HEADS):
            cs = slice(h * DH, (h + 1) * DH)
            s = sret_ref[j, h]
            kcol = _to_col(pick(kr_s, cs), eye_mask)
            qcol = _to_col(pick(qr_s, cs), eye_mask)
            sn = s * gam_ref[h] + kcol * pick(vr_ref, cs)
            sret_o[j, h] = sn
            o = jnp.sum(qcol * sn, axis=0, keepdims=True)
            mu = jnp.mean(o, axis=-1, keepdims=True)
            d = o - mu
            var = jnp.mean(d * d, axis=-1, keepdims=True)
            on = d * lax.rsqrt(var + LN_EPS)
            pieces_r.append(_silu(pick(gr_ref, cs)) * (on * gnw_ref[:, cs] + gnb_ref[:, cs]))
            s = sgdn_ref[j, h]
            eg = jnp.exp(gb[:, h:h + 1])
            beta = gb[:, HEADS + h:HEADS + h + 1]
            kcol = _to_col(pick(kd_s, cs), eye_mask)
            qcol = _to_col(pick(qd_s, cs), eye_mask)
            ks = jnp.sum(kcol * s, axis=0, keepdims=True)
            v_new = beta * (pick(vd_s, cs) - eg * ks)
            sn = s * eg + kcol * v_new
            sgdn_o[j, h] = sn
            o = jnp.sum(qcol * sn, axis=0, keepdims=True)
            o = o * lax.rsqrt(jnp.mean(o * o, axis=-1, keepdims=True) + NORM_EPS) * nw_ref[...]
            pieces_d.append(o * _silu(pick(zd_ref, cs)))
        orow = jnp.concatenate(pieces_r + pieces_d, axis=1)
        return jnp.where(lax.broadcasted_iota(I32, o8.shape, 0) == j, orow, o8)

    o_scr[rows8, :] = lax.fori_loop(0, bb, per_seq, jnp.zeros((bb, 2 * HW), F32))

    @pl.when(step == pl.num_programs(0) - 1)
    def _():
        o_ref[...] = o_scr[...].astype(o_ref.dtype)


def _mixer_sample(p, o, n_prompt, dbs, s_ret, s_gdn, s_conv_t, offset, cw, alog_row, dtb_row, gnw, gnb, nw):
    bb = SUBLANES
    n = p.shape[0]
    rb = n_prompt // dbs
    cos2, sin2 = _rope_tables(1, offset)
    log_gamma = jnp.log(1.0 - 2.0 ** (-5.0 - jnp.arange(HEADS, dtype=F32)))
    gam = jnp.broadcast_to(jnp.exp(log_gamma)[:, None, None], (HEADS, 1, LANES)).astype(F32)
    col = lambda c: pl.BlockSpec((dbs, HW), lambda s, c=c: (rb, c))
    full = lambda a: pl.BlockSpec(a.shape, lambda s: (0,) * a.ndim)
    st = pl.BlockSpec((bb, HEADS, DH, DH), lambda s: (s, 0, 0, 0))
    vec = lambda: pltpu.VMEM((dbs, HW), F32)
    return pl.pallas_call(
        functools.partial(_dec_body, bb=bb),
        out_shape=(jax.ShapeDtypeStruct((n, 2 * HW), BF16),
                   jax.ShapeDtypeStruct(s_ret.shape, F32),
                   jax.ShapeDtypeStruct(s_gdn.shape, F32),
                   jax.ShapeDtypeStruct(s_conv_t.shape, F32)),
        grid=(dbs // bb,),
        in_specs=[col(COL_QR), col(COL_KR), col(COL_VR), col(COL_GR),
                  col(COL_QD), col(COL_KD), col(COL_VD), col(COL_ZD),
                  pl.BlockSpec((dbs, LANES), lambda s: (rb, TAIL_COL // LANES)),
                  st, st, full(s_conv_t),
                  pl.BlockSpec(memory_space=pl.ANY),
                  full(cos2), full(sin2), full(gam), full(cw), full(alog_row), full(dtb_row),
                  full(gnw), full(gnb), full(nw)],
        out_specs=(pl.BlockSpec((dbs, 2 * HW), lambda s: (rb, 0)), st, st, full(s_conv_t)),
        scratch_shapes=[vec(), vec(), vec(), vec(), vec(),
                        pltpu.VMEM((dbs, LANES), F32), pltpu.VMEM((dbs, 2 * HW), F32)],
        input_output_aliases={12: 0},
        compiler_params=_cparams(("arbitrary",)),
        name="mixer_sample",
    )(p, p, p, p, p, p, p, p, p, s_ret, s_gdn, s_conv_t, o, cos2, sin2, gam, cw, alog_row, dtb_row,
      gnw, gnb, nw)


def _post_mixer_body(o_ref, x_ref, wout_ref, g_ref, b_ref, wr_ref, br_ref,
                     x1_ref, x1t_ref, e4_ref, r4_ref, g4_ref, cnt_ref, carry_scr, *, alpha, tm):
    i = pl.program_id(0)

    @pl.when(i == 0)
    def _():
        carry_scr[...] = jnp.zeros_like(carry_scr)

    y = _dot(o_ref[...], wout_ref[...])
    x1 = _layer_norm(alpha * x_ref[...] + y, g_ref[...], b_ref[...])
    x1_ref[...] = x1
    _rows_to_tiles(x1t_ref, x1)
    logits = _dot(x1.astype(BF16), wr_ref[...]) + br_ref[...]
    lane = lax.broadcasted_iota(I32, (tm, LANES), 1)
    lane_f = lane.astype(F32)
    work = logits
    ohs, vals, idxs = [], [], []
    for _ in range(TOP_K):
        m = jnp.max(work, axis=1, keepdims=True)
        idx = jnp.min(jnp.where(work == m, lane_f, float(LANES)), axis=1, keepdims=True)
        oh = lane_f == idx
        ohs.append(oh)
        vals.append(m)
        idxs.append(idx)
        work = jnp.where(oh, NEG, work)
    es = [jnp.exp(v - vals[0]) for v in vals]
    den = es[0] + es[1] + es[2] + es[3]
    sel = jnp.zeros((tm, LANES), F32)
    for oh in ohs:
        sel = sel + jnp.where(oh, 1.0, 0.0)
    r_i = lax.broadcasted_iota(I32, (tm, tm), 0)
    c_i = lax.broadcasted_iota(I32, (tm, tm), 1)
    ltri = jnp.where(r_i >= c_i, 1.0, 0.0).astype(BF16)
    incl = _dot(ltri, sel.astype(BF16))
    excl = incl - sel + carry_scr[0:1, :]
    carry_scr[...] = carry_scr[...] + incl[tm - 1:tm, :]
    e4 = jnp.zeros((tm, LANES), F32)
    r4 = jnp.zeros((tm, LANES), F32)
    g4 = jnp.zeros((tm, LANES), F32)
    for k in range(TOP_K):
        rk = jnp.sum(jnp.where(ohs[k], excl, 0.0), axis=1, keepdims=True)
        e4 = jnp.where(lane == k, idxs[k], e4)
        r4 = jnp.where(lane == k, rk, r4)
        g4 = jnp.where(lane == k, es[k] / den, g4)
    e4_ref[...] = e4.astype(I32)
    r4_ref[...] = r4.astype(I32)
    g4_ref[...] = g4
    cnt_ref[...] = carry_scr[...]


def _post_mixer(o, x, wout_bf, ln_g, ln_b, wr_bf, br_row, alpha, tm):
    n, d = x.shape
    row = lambda w: pl.BlockSpec((tm, w), lambda i: (i, 0))
    full = lambda a: pl.BlockSpec(a.shape, lambda i: (0,) * a.ndim)
    return pl.pallas_call(
        functools.partial(_post_mixer_body, alpha=alpha, tm=tm),
        out_shape=(jax.ShapeDtypeStruct((n, d), F32),
                   jax.ShapeDtypeStruct((n * SUBLANES, LANES), F32),
                   jax.ShapeDtypeStruct((n, LANES), I32),
                   jax.ShapeDtypeStruct((n, LANES), I32),
                   jax.ShapeDtypeStruct((n, LANES), F32),
                   jax.ShapeDtypeStruct((SUBLANES, LANES), F32)),
        grid=(n // tm,),
        in_specs=[row(2 * HW), row(d), full(wout_bf), full(ln_g), full(ln_b), full(wr_bf), full(br_row)],
        out_specs=(row(d), pl.BlockSpec((tm * SUBLANES, LANES), lambda i: (i, 0)),
                   row(LANES), row(LANES), row(LANES),
                   pl.BlockSpec((SUBLANES, LANES), lambda i: (0, 0))),
        scratch_shapes=[pltpu.VMEM((SUBLANES, LANES), F32)],
        compiler_params=_cparams(("arbitrary",)),
        name="post_mixer",
    )(o, x, wout_bf, ln_g, ln_b, wr_bf, br_row)


def _plan_body(e4_ref, r4_ref, cnt_ref, d4_ref, blk_ref, seg_ref, pstart_scr, *, tm, n_experts, rows):
    nbp = blk_ref.shape[0]

    @pl.when(pl.program_id(0) == 0)
    def _():
        cnt = cnt_ref[...]
        padded = jnp.floor((cnt + (MOE_BM - 1.0)) * (1.0 / MOE_BM)) * MOE_BM
        m_i = lax.broadcasted_iota(I32, (LANES, LANES), 0)
        j_i = lax.broadcasted_iota(I32, (LANES, LANES), 1)
        upper = jnp.where(m_i <= j_i, 1.0, 0.0).astype(F32)
        pad_end = _dot(padded, upper, lax.Precision.HIGHEST)
        pstart_scr[...] = pad_end - padded
        bstart = lax.broadcasted_iota(I32, (nbp, LANES), 0).astype(F32) * MOE_BM
        blane = lax.broadcasted_iota(I32, (nbp, LANES), 1)
        hit = jnp.where((pad_end[0:1, :] <= bstart) & (blane < n_experts), 1.0, 0.0)
        be = jnp.minimum(jnp.sum(hit, axis=1, keepdims=True), n_experts - 1.0)
        blk_ref[...] = jnp.broadcast_to(be, (nbp, LANES)).astype(I32)
        srow = lax.broadcasted_iota(I32, (SUBLANES, LANES), 0)
        slane = lax.broadcasted_iota(I32, (SUBLANES, LANES), 1)
        total = jnp.sum(jnp.where(slane == n_experts - 1, pad_end, 0.0), axis=1, keepdims=True)
        seg_end = jnp.where(slane == n_experts - 1, float(rows), pad_end)
        seg = jnp.where(srow == 0, pad_end - padded + cnt,
                        jnp.where(srow == 1, seg_end,
                                  jnp.where(srow == 2, total * (1.0 / MOE_BM), pad_end * (1.0 / MOE_BM))))
        seg_ref[...] = seg.astype(I32)

    pad_start = pstart_scr[0:1, :]
    lane = lax.broadcasted_iota(I32, (tm, LANES), 1)
    e4 = e4_ref[...]
    r4 = r4_ref[...].astype(F32)
    d4 = jnp.zeros((tm, LANES), F32)
    for k in range(TOP_K):
        ek = e4[:, k:k + 1]
        ps = jnp.sum(jnp.where(lane == ek, pad_start, 0.0), axis=1, keepdims=True)
        d4 = jnp.where(lane == k, ps + r4[:, k:k + 1], d4)
    d4_ref[...] = d4.astype(I32)


def _plan(e4, r4, cnt, tm, n_experts, rows):
    n = e4.shape[0]
    nbp = pl.cdiv(rows // MOE_BM, SUBLANES) * SUBLANES
    row = pl.BlockSpec((tm, LANES), lambda i: (i, 0))
    return pl.pallas_call(
        functools.partial(_plan_body, tm=tm, n_experts=n_experts, rows=rows),
        out_shape=(jax.ShapeDtypeStruct((n, LANES), I32),
                   jax.ShapeDtypeStruct((nbp, LANES), I32),
                   jax.ShapeDtypeStruct((SUBLANES, LANES), I32)),
        grid=(n // tm,),
        in_specs=[row, row, pl.BlockSpec((SUBLANES, LANES), lambda i: (0, 0))],
        out_specs=(row, pl.BlockSpec((nbp, LANES), lambda i: (0, 0)),
                   pl.BlockSpec((SUBLANES, LANES), lambda i: (0, 0))),
        scratch_shapes=[pltpu.VMEM((SUBLANES, LANES), F32)],
        compiler_params=_cparams(("arbitrary",)),
        name="moe_plan",
    )(e4, r4, cnt)


def _tile_at(ref, r, lead=()):
    return ref.at[lead + (pl.ds(pl.multiple_of(r * SUBLANES, SUBLANES), SUBLANES), slice(None))]


def _issue(copy_of, count):
    def start(i, carry):
        for u in range(DMA_UNROLL):
            copy_of(i * DMA_UNROLL + u).start(priority=u % 2)
        return carry

    lax.fori_loop(0, count // DMA_UNROLL, start, 0)


def _wait(copy_of, count):
    def wait(pidx, carry):
        copy_of(pidx).wait()
        return carry

    lax.fori_loop(0, count, wait, 0, unroll=DMA_UNROLL)


def _dispatch_body(zs_ref, ze_ref, dest_ref, x_ref, xs_ref, zero_scr, sem, zsem, *, n_experts):
    def token_copy(pidx):
        t = lax.shift_right_logical(pidx, 2)
        return pltpu.make_async_copy(_tile_at(x_ref, t), _tile_at(xs_ref, dest_ref[0, 0, pidx]), sem)

    _issue(token_copy, TOK_TILE * TOP_K)
    _wait(token_copy, TOK_TILE * TOP_K)

    @pl.when(pl.program_id(0) == pl.num_programs(0) - 1)
    def _():
        zero_scr[...] = jnp.zeros_like(zero_scr)

        def zero_copy(r):
            return pltpu.make_async_copy(zero_scr, _tile_at(xs_ref, r), zsem)

        def per_expert(wait):
            def body(e, carry):
                def one(r, c):
                    zero_copy(r).wait() if wait else zero_copy(r).start()
                    return c
                lax.fori_loop(zs_ref[e], ze_ref[e], one, 0)
                return carry
            lax.fori_loop(0, n_experts, body, 0)

        per_expert(False)
        per_expert(True)


def _dispatch(x1t, dest3, zstart, zend, rows, n_experts):
    n = x1t.shape[0] // SUBLANES
    npairs = TOK_TILE * TOP_K
    grid_spec = pltpu.PrefetchScalarGridSpec(
        num_scalar_prefetch=2,
        grid=(n // TOK_TILE,),
        in_specs=[pl.BlockSpec((1, 1, npairs), lambda i, zs, ze: (i, 0, 0), memory_space=pltpu.SMEM),
                  pl.BlockSpec((TOK_TILE * SUBLANES, LANES), lambda i, zs, ze: (i, 0))],
        out_specs=pl.BlockSpec(memory_space=pl.ANY),
        scratch_shapes=[pltpu.VMEM((SUBLANES, LANES), F32), pltpu.SemaphoreType.DMA(()),
                        pltpu.SemaphoreType.DMA(())],
    )
    return pl.pallas_call(
        functools.partial(_dispatch_body, n_experts=n_experts),
        out_shape=jax.ShapeDtypeStruct((rows * SUBLANES, LANES), F32),
        grid_spec=grid_spec,
        compiler_params=_cparams(("arbitrary",)),
        name="moe_dispatch",
    )(zstart, zend, dest3, x1t)


def _ffn_body(blk_ref, nused_ref, segend_ref, xs_ref, bgu_ref, bd_ref, wgu_hbm, wd_hbm, ys_ref,
              wgu_f32, wd_f32, wgu_bf, wd_bf, sems, slot_ref, *, de, layer):
    i = pl.program_id(0)
    nused = nused_ref[0]
    e = blk_ref[i]
    prev = blk_ref[jnp.maximum(i - 1, 0)]

    def weight_copies(expert, slot):
        return (pltpu.make_async_copy(wgu_hbm.at[layer, expert], wgu_f32.at[slot], sems.at[0, slot]),
                pltpu.make_async_copy(wd_hbm.at[layer, expert], wd_f32.at[slot], sems.at[1, slot]))

    @pl.when((i == 0) & (nused > 0))
    def _():
        slot_ref[0] = 0
        for c in weight_copies(e, 0):
            c.start()

    @pl.when((i < nused) & ((i == 0) | (e != prev)))
    def _():
        slot = slot_ref[0]
        for c in weight_copies(e, slot):
            c.wait()
        nxt = segend_ref[e]

        @pl.when(nxt < nused)
        def _():
            for c in weight_copies(blk_ref[nxt], 1 - slot):
                c.start()

        rb = 128
        for r in range(0, wgu_bf.shape[0], rb):
            wgu_bf[r:r + rb, :] = wgu_f32[slot, r:r + rb, :].astype(BF16)
        for r in range(0, wd_bf.shape[0], rb):
            wd_bf[r:r + rb, :] = wd_f32[slot, r:r + rb, :].astype(BF16)
        slot_ref[0] = 1 - slot

    @pl.when(i < nused)
    def _():
        xb = _tiles_to_rows(xs_ref, MOE_BM).astype(BF16)
        gu = _dot(xb, wgu_bf[...]) + bgu_ref[...]
        gate = jnp.minimum(gu[:, :de], SWIGLU_LIMIT)
        up = jnp.clip(gu[:, de:], -SWIGLU_LIMIT, SWIGLU_LIMIT)
        act = (up + 1.0) * gate * jax.nn.sigmoid(SWIGLU_ALPHA * gate)
        _rows_to_tiles(ys_ref, _dot(act.astype(BF16), wd_bf[...]) + bd_ref[...])

    @pl.when(i >= nused)
    def _():
        ys_ref[...] = jnp.zeros_like(ys_ref)


def _ffn(xs, blk_e, nused, segend_blk, w_gu, b_gu, w_down, b_down, layer):
    rows = xs.shape[0] // SUBLANES
    depth, n_experts, d, de2 = w_gu.shape
    de = de2 // 2
    nb = rows // MOE_BM
    tile_blk = pl.BlockSpec((MOE_BM * SUBLANES, LANES), lambda i, blk, nu, se: (i, 0))
    grid_spec = pltpu.PrefetchScalarGridSpec(
        num_scalar_prefetch=3,
        grid=(nb,),
        in_specs=[tile_blk,
                  pl.BlockSpec((None, None, 1, de2), lambda i, blk, nu, se: (layer, blk[i], 0, 0)),
                  pl.BlockSpec((None, None, 1, d), lambda i, blk, nu, se: (layer, blk[i], 0, 0)),
                  pl.BlockSpec(memory_space=pl.ANY),
                  pl.BlockSpec(memory_space=pl.ANY)],
        out_specs=tile_blk,
        scratch_shapes=[pltpu.VMEM((2, d, de2), F32), pltpu.VMEM((2, de, d), F32),
                        pltpu.VMEM((d, de2), BF16), pltpu.VMEM((de, d), BF16),
                        pltpu.SemaphoreType.DMA((2, 2)), pltpu.SMEM((1,), I32)],
    )
    return pl.pallas_call(
        functools.partial(_ffn_body, de=de, layer=layer),
        out_shape=jax.ShapeDtypeStruct((rows * SUBLANES, LANES), F32),
        grid_spec=grid_spec,
        compiler_params=_cparams(("arbitrary",)),
        name="moe_ffn",
    )(blk_e, nused, segend_blk, xs, b_gu.reshape(depth, n_experts, 1, de2),
      b_down.reshape(depth, n_experts, 1, d), w_gu, w_down)


def _combine_body(dest_ref, dest_next_ref, g4_ref, x1_ref, g_ref, b_ref, ys_ref, *rest, alpha, split_tiles):
    *out_refs, buf, sems = rest
    i = pl.program_id(0)
    slot = lax.rem(i, 2)
    npairs = TOK_TILE * TOP_K

    def gather(dref, s):
        def row_copy(pidx):
            t = lax.shift_right_logical(pidx, 2)
            k = lax.bitwise_and(pidx, TOP_K - 1)
            return pltpu.make_async_copy(_tile_at(ys_ref, dref[0, 0, pidx]), _tile_at(buf, t, (s, k)), sems.at[s])
        return row_copy

    @pl.when(i == 0)
    def _():
        _issue(gather(dest_ref, 0), npairs)

    @pl.when(i + 1 < pl.num_programs(0))
    def _():
        _issue(gather(dest_next_ref, 1 - slot), npairs)

    _wait(gather(dest_ref, slot), npairs)
    g4 = g4_ref[...]
    f = g4[:, 0:1] * _tiles_to_rows(buf, TOK_TILE, (slot, 0))
    for k in range(1, TOP_K):
        f = f + g4[:, k:k + 1] * _tiles_to_rows(buf, TOK_TILE, (slot, k))
    res = _layer_norm(alpha * x1_ref[...] + f, g_ref[...], b_ref[...])
    if split_tiles is None:
        out_refs[0][...] = res
    else:

        @pl.when(i < split_tiles)
        def _():
            out_refs[0][...] = res

        @pl.when(i >= split_tiles)
        def _():
            out_refs[1][...] = res


def _combine(ys, dest3, g4, x1, ln_g, ln_b, alpha, n_prompt=None):
    n, d = x1.shape
    full = lambda a: pl.BlockSpec(a.shape, lambda i: (0,) * a.ndim)
    tile = lambda f: pl.BlockSpec((TOK_TILE, d), f)
    if n_prompt is None:
        split_tiles = None
        out_shape = jax.ShapeDtypeStruct((n, d), F32)
        out_specs = tile(lambda i: (i, 0))
    else:
        assert n - n_prompt == TOK_TILE and n_prompt % TOK_TILE == 0
        split_tiles = n_prompt // TOK_TILE
        out_shape = (jax.ShapeDtypeStruct((n_prompt, d), F32), jax.ShapeDtypeStruct((TOK_TILE, d), F32))
        out_specs = (tile(lambda i: (jnp.minimum(i, split_tiles - 1), 0)), tile(lambda i: (0, 0)))
    nsteps = n // TOK_TILE
    dest_spec = lambda f: pl.BlockSpec((1, 1, TOK_TILE * TOP_K), f, memory_space=pltpu.SMEM)
    return pl.pallas_call(
        functools.partial(_combine_body, alpha=alpha, split_tiles=split_tiles),
        out_shape=out_shape,
        grid=(nsteps,),
        in_specs=[dest_spec(lambda i: (i, 0, 0)),
                  dest_spec(lambda i: (jnp.minimum(i + 1, nsteps - 1), 0, 0)),
                  pl.BlockSpec((TOK_TILE, LANES), lambda i: (i, 0)),
                  tile(lambda i: (i, 0)),
                  full(ln_g), full(ln_b),
                  pl.BlockSpec(memory_space=pl.ANY)],
        out_specs=out_specs,
        scratch_shapes=[pltpu.VMEM((2, TOP_K, TOK_TILE * SUBLANES, LANES), F32), pltpu.SemaphoreType.DMA((2,))],
        compiler_params=_cparams(("arbitrary",)),
        name="moe_combine",
    )(dest3, dest3, g4, x1, ln_g, ln_b, ys)


def kernel(x_prompt, x_sample, state_ret, state_gdn, state_conv, w_in, conv_w, a_log, dt_bias, ret_gn_w, ret_gn_b, gdn_norm_w, w_out, ln1_g, ln1_b, w_router, b_router, w_gu, b_gu, w_down, b_down, ln2_g, ln2_b):
    bsz, seq, d = x_prompt.shape
    dbs, dseq, _ = x_sample.shape
    depth = w_in.shape[0]
    n_experts = w_router.shape[-1]
    assert dseq == 1 and dbs == LANES and seq % CHUNK == 0 and d == 2 * HW == SUBLANES * LANES
    assert 2 ** (INV_BASE_LEVELS + 1) == INV_BASE and CHUNK % INV_BASE == 0
    n_prompt = bsz * seq
    n = n_prompt + dbs
    alpha = (2.0 * depth) ** 0.25
    tm = _pick_tile(n, (384, 256, 128))
    tt = _pick_tile(seq, (512, 256, 128))
    rows = (pl.cdiv(n * TOP_K, MOE_BM) + n_experts) * MOE_BM

    x = jnp.concatenate([x_prompt.reshape(n_prompt, d), x_sample.reshape(dbs, d)], axis=0)
    row2 = lambda a: a.reshape(1, -1).astype(F32)
    pad_lanes = lambda a, fill: jnp.concatenate(
        [a.astype(F32), jnp.full((LANES - a.shape[0],), fill, F32)]).reshape(1, LANES)

    ret_p, gdn_p, conv_p, ret_s, gdn_s, conv_s = [], [], [], [], [], []
    for l in range(depth):
        w_in_bf = jnp.pad(w_in[l], ((0, 0), (0, IN_PAD - w_in.shape[-1]))).astype(BF16)
        p = _inproj(x, w_in_bf, tm)
        gnw, gnb, nw = row2(ret_gn_w[l]), row2(ret_gn_b[l]), row2(gdn_norm_w[l])
        alog_row, dtb_row = pad_lanes(a_log[l], 0.0), pad_lanes(dt_bias[l], 0.0)
        cw = conv_w[l].astype(F32)

        o, sr = _retention_prompt(p, n, bsz, seq, tt, gnw, gnb)
        o, sg, ctail = _gdn_prompt(p, o, bsz, seq, tt, cw, alog_row, dtb_row, nw)
        sconv_t = jnp.transpose(state_conv[l], (1, 0, 2))
        o, sr_s, sg_s, sc_s = _mixer_sample(p, o, n_prompt, dbs, state_ret[l], state_gdn[l], sconv_t,
                                            float(PAST_LEN), cw, alog_row, dtb_row, gnw, gnb, nw)
        ret_p.append(sr)
        gdn_p.append(sg)
        conv_p.append(jnp.transpose(ctail[:, :, SUBLANES - (CONV_W - 1):, :], (0, 2, 1, 3))
                      .reshape(bsz, CONV_W - 1, 3 * HW))
        ret_s.append(sr_s)
        gdn_s.append(sg_s)
        conv_s.append(jnp.transpose(sc_s, (1, 0, 2)))

        wr_bf = jnp.pad(w_router[l], ((0, 0), (0, LANES - n_experts))).astype(BF16)
        br_row = pad_lanes(b_router[l], NEG)
        x1, x1t, e4, r4, g4, cnt = _post_mixer(o, x, w_out[l].astype(BF16), row2(ln1_g[l]), row2(ln1_b[l]),
                                               wr_bf, br_row, alpha, tm)
        d4, blk, seg = _plan(e4, r4, cnt, tm, n_experts, rows)
        dest3 = d4[:, :TOP_K].reshape(n // TOK_TILE, 1, TOK_TILE * TOP_K)
        xs = _dispatch(x1t, dest3, seg[0, :n_experts], seg[1, :n_experts], rows, n_experts)
        ys = _ffn(xs, blk[:rows // MOE_BM, 0], seg[2, :1], seg[3, :n_experts], w_gu, b_gu, w_down, b_down, l)
        if l + 1 < depth:
            x = _combine(ys, dest3, g4, x1, row2(ln2_g[l]), row2(ln2_b[l]), alpha)
        else:
            y_prompt, y_sample = _combine(ys, dest3, g4, x1, row2(ln2_g[l]), row2(ln2_b[l]), alpha, n_prompt)

    return (y_prompt.reshape(bsz, seq, d), y_sample.reshape(dbs, dseq, d),
            jnp.stack(ret_p), jnp.stack(gdn_p), jnp.stack(conv_p),
            jnp.stack(ret_s), jnp.stack(gdn_s), jnp.stack(conv_s))
```

```python
import functools

import jax
import jax.numpy as jnp
from jax import lax
from jax.experimental import pallas as pl
from jax.experimental.pallas import tpu as pltpu

F32 = jnp.float32
BF16 = jnp.bfloat16
I32 = jnp.int32

HEADS = 4
DH = 128
HW = HEADS * DH
CONV_W = 4
ROPE_BASE = 10000.0
PAST_LEN = 16384
TOP_K = 4
SWIGLU_LIMIT = 7.0
SWIGLU_ALPHA = 1.702
LN_EPS = 1e-5
NORM_EPS = 1e-6
L2_EPS = 1e-6

LANES = 128
SUBLANES = 8
VMEM_LIMIT_BYTES = 56 * 1024 * 1024

CHUNK = 128
GDN_GROUP = 2
INV_BASE = 16
INV_BASE_LEVELS = 3
MOE_BM = 256
TOK_TILE = 128
ZERO_ROWS = 128
PLAN_TILE_CAP = 8192
DMA_UNROLL = 8
NEG = -3.0e38

COL_QR, COL_KR, COL_VR, COL_GR, COL_QD, COL_KD, COL_VD, COL_ZD = range(8)
TAIL_COL = 8 * HW
IN_PAD = TAIL_COL + LANES

_NT = (((1,), (1,)), ((), ()))


def _cparams(sem, vmem=VMEM_LIMIT_BYTES):
    return pltpu.CompilerParams(dimension_semantics=sem, vmem_limit_bytes=vmem)


def _dot(a, b, precision=None):
    return jnp.dot(a, b, preferred_element_type=F32, precision=precision)


def _dot_nt(a, b):
    return lax.dot_general(a, b, _NT, preferred_element_type=F32)


def _silu(x):
    return x * jax.nn.sigmoid(x)


def _layer_norm(h, g, b):
    mu = jnp.mean(h, axis=-1, keepdims=True)
    d = h - mu
    var = jnp.mean(d * d, axis=-1, keepdims=True)
    return d * lax.rsqrt(var + LN_EPS) * g + b


def _rope(x, cos2, sin2):
    return x * cos2 + pltpu.roll(x, DH // 2, 1) * sin2


def _tiles_to_rows(ref, nrows, lead=()):
    return jnp.concatenate([ref[lead + (pl.ds(s, nrows, stride=SUBLANES), slice(None))] for s in range(SUBLANES)],
                           axis=1)


def _rows_to_tiles(ref, val):
    nrows = val.shape[0]
    for s in range(SUBLANES):
        ref[pl.ds(s, nrows, stride=SUBLANES), :] = val[:, s * LANES:(s + 1) * LANES]


def _pick_tile(n, candidates):
    for c in candidates:
        if n % c == 0:
            return c
    raise ValueError(f"no tile in {candidates} divides {n}")


def _largest_tile(n, cap):
    for parts in range(1, n + 1):
        if n % parts == 0 and (n // parts) % SUBLANES == 0 and n // parts <= cap:
            return n // parts
    raise ValueError(f"no tile of at most {cap} rows divides {n}")


def _inproj_body(x_ref, w_ref, o_ref):
    o_ref[...] = _dot(x_ref[...].astype(BF16), w_ref[...])


def _inproj(x, w_bf, tm):
    n, d = x.shape
    return pl.pallas_call(
        _inproj_body,
        out_shape=jax.ShapeDtypeStruct((n, IN_PAD), F32),
        grid=(n // tm,),
        in_specs=[pl.BlockSpec((tm, d), lambda i: (i, 0)),
                  pl.BlockSpec((d, IN_PAD), lambda i: (0, 0))],
        out_specs=pl.BlockSpec((tm, IN_PAD), lambda i: (i, 0)),
        compiler_params=_cparams(("arbitrary",)),
        name="inproj",
    )(x, w_bf)


def _ret_body(q_ref, k_ref, v_ref, g_ref, o_in_ref, cos_ref, sin_ref, intra_ref, qdec_ref, kdec_ref, cdec_ref,
              gnw_ref, gnb_ref, o_ref, sout_ref, s_scr, *, nchunk):
    del o_in_ref
    tb = pl.program_id(1)

    @pl.when(tb == 0)
    def _():
        s_scr[...] = jnp.zeros_like(s_scr)

    def chunk(ci, carry):
        rows = pl.ds(pl.multiple_of(ci * CHUNK, CHUNK), CHUNK)
        cos2 = cos_ref[rows, :]
        sin2 = sin_ref[rows, :]
        hs = range(HEADS)
        cols = [slice(h * DH, (h + 1) * DH) for h in hs]
        q = [_rope(q_ref[rows, cols[h]], cos2, sin2) for h in hs]
        k = [_rope(k_ref[rows, cols[h]], cos2, sin2) * (DH ** -0.5) for h in hs]
        vb = [v_ref[rows, cols[h]].astype(BF16) for h in hs]
        s = [s_scr[h] for h in hs]
        att = [_dot_nt(q[h].astype(BF16), k[h].astype(BF16)) * intra_ref[h] for h in hs]
        qs = [_dot((q[h] * qdec_ref[h]).astype(BF16), s[h].astype(BF16)) for h in hs]
        av = [_dot(jnp.concatenate([att[h], (k[h] * kdec_ref[h]).T], axis=0).astype(BF16), vb[h]) for h in hs]
        for h in hs:
            s_scr[h] = s[h] * cdec_ref[h] + av[h][CHUNK:]
            o = av[h][:CHUNK] + qs[h]
            mu = jnp.mean(o, axis=-1, keepdims=True)
            d = o - mu
            var = jnp.mean(d * d, axis=-1, keepdims=True)
            on = d * lax.rsqrt(var + LN_EPS)
            res = _silu(g_ref[rows, cols[h]]) * (on * gnw_ref[:, cols[h]] + gnb_ref[:, cols[h]])
            o_ref[rows, cols[h]] = res.astype(o_ref.dtype)
        return carry

    lax.fori_loop(0, nchunk, chunk, 0)

    @pl.when(tb == pl.num_programs(1) - 1)
    def _():
        sout_ref[0] = s_scr[...]


def _ret_tables(c):
    log_gamma = jnp.log(1.0 - 2.0 ** (-5.0 - jnp.arange(HEADS, dtype=F32)))
    idx = jnp.arange(c, dtype=F32)
    diff = idx[:, None] - idx[None, :]
    causal = diff >= 0
    intra = jnp.where(causal, jnp.exp(log_gamma[:, None, None] * jnp.where(causal, diff, 0.0)), 0.0)
    q_dec = jnp.exp(log_gamma[:, None] * (idx + 1.0))[..., None]
    k_dec = jnp.exp(log_gamma[:, None] * (c - 1.0 - idx))[..., None]
    c_dec = jnp.exp(log_gamma * c)[:, None, None]
    bc = lambda a, r: jnp.broadcast_to(a, (HEADS, r, LANES)).astype(F32)
    return intra, bc(q_dec, c), bc(k_dec, c), bc(c_dec, 1)


def _rope_tables(t, offset):
    half = DH // 2
    inv_freq = ROPE_BASE ** (-jnp.arange(half, dtype=F32) / half)
    pos = jnp.arange(t, dtype=F32) + offset
    ang = pos[:, None] * inv_freq[None, :]
    cos, sin = jnp.cos(ang), jnp.sin(ang)
    return jnp.concatenate([cos, cos], -1), jnp.concatenate([-sin, sin], -1)


def _retention_prompt(p, n, bsz, seq, tt, gnw, gnb):
    nchunk = tt // CHUNK
    nt = seq // tt
    cos2, sin2 = _rope_tables(seq, 0)
    intra, qdec, kdec, cdec = _ret_tables(CHUNK)
    col = lambda c: pl.BlockSpec((tt, HW), lambda b, t, c=c: (b * nt + t, c))
    full = lambda a: pl.BlockSpec(a.shape, lambda b, t: (0,) * a.ndim)
    return pl.pallas_call(
        functools.partial(_ret_body, nchunk=nchunk),
        out_shape=(jax.ShapeDtypeStruct((n, 2 * HW), BF16),
                   jax.ShapeDtypeStruct((bsz, HEADS, DH, DH), F32)),
        grid=(bsz, nt),
        in_specs=[col(COL_QR), col(COL_KR), col(COL_VR), col(COL_GR),
                  pl.BlockSpec(memory_space=pl.ANY),
                  pl.BlockSpec((tt, DH), lambda b, t: (t, 0)),
                  pl.BlockSpec((tt, DH), lambda b, t: (t, 0)),
                  full(intra), full(qdec), full(kdec), full(cdec), full(gnw), full(gnb)],
        out_specs=(pl.BlockSpec((tt, HW), lambda b, t: (b * nt + t, 0)),
                   pl.BlockSpec((1, HEADS, DH, DH), lambda b, t: (b, 0, 0, 0))),
        scratch_shapes=[pltpu.VMEM((HEADS, DH, DH), F32)],
        input_output_aliases={4: 0},
        compiler_params=_cparams(("arbitrary", "arbitrary")),
        name="retention_prompt",
    )(p, p, p, p, jnp.zeros((n, 2 * HW), BF16), cos2, sin2, intra, qdec, kdec, cdec, gnw, gnb)


def _softplus(x):
    return jnp.maximum(x, 0.0) + jnp.log1p(jnp.exp(-jnp.abs(x)))


def _gdn_body(qd_ref, kd_ref, vd_ref, z_ref, tail_ref, o_in_ref, cw_ref, alog_ref, dtb_ref, nw_ref,
              o_ref, sout_ref, ctail_ref, s_scr, prev_scr, xe_scr, qn_scr, kn_scr, vn_scr, g_scr, b_scr,
              *, nchunk, tt):
    del o_in_ref
    tb = pl.program_id(1)

    @pl.when(tb == 0)
    def _():
        s_scr[...] = jnp.zeros_like(s_scr)
        prev_scr[...] = jnp.zeros_like(prev_scr)

    for gi, (src, dst) in enumerate(((qd_ref, qn_scr), (kd_ref, kn_scr), (vd_ref, vn_scr))):
        gcols = slice(gi * HW, (gi + 1) * HW)
        xe_scr[0:SUBLANES, :] = prev_scr[gi]
        xe_scr[SUBLANES:SUBLANES + tt, :] = src[...]
        prev_scr[gi] = src[tt - SUBLANES:tt, :]
        for c in range(nchunk):
            base = SUBLANES + c * CHUNK
            acc = xe_scr[base:base + CHUNK, :] * cw_ref[CONV_W - 1:CONV_W, gcols]
            for j in range(1, CONV_W):
                acc = acc + xe_scr[base - j:base - j + CHUNK, :] * cw_ref[CONV_W - 1 - j:CONV_W - j, gcols]
            y = _silu(acc)
            for h in range(HEADS):
                cs = slice(h * DH, (h + 1) * DH)
                seg = y[:, cs]
                if gi < 2:
                    seg = seg * lax.rsqrt(jnp.sum(seg * seg, axis=-1, keepdims=True) + L2_EPS)
                if gi == 0:
                    seg = seg * (DH ** -0.5)
                dst[c * CHUNK:(c + 1) * CHUNK, cs] = seg

    tail = tail_ref[...]
    g_scr[...] = -jnp.exp(alog_ref[...]) * _softplus(tail + dtb_ref[...])
    b_scr[...] = jax.nn.sigmoid(tail)

    ri = lax.broadcasted_iota(I32, (CHUNK, CHUNK), 0)
    ci_ = lax.broadcasted_iota(I32, (CHUNK, CHUNK), 1)
    causal = ri >= ci_
    strict = ri > ci_
    ltri = jnp.where(causal, 1.0, 0.0).astype(F32)
    same_block = lambda size: (lax.shift_right_logical(ri, size.bit_length() - 1)
                               == lax.shift_right_logical(ci_, size.bit_length() - 1))
    base_mask = same_block(INV_BASE)
    off_masks = []
    size = INV_BASE
    while size < CHUNK:
        off_masks.append(same_block(2 * size) & jnp.logical_not(same_block(size)))
        size *= 2

    def chunk_group(gi, carry):
        rows_c, gcum_c, gcum_t_c, bt_c = [], [], [], []
        for c in range(GDN_GROUP):
            rows_c.append(pl.ds(pl.multiple_of((gi * GDN_GROUP + c) * CHUNK, CHUNK), CHUNK))
            gcum_c.append(_dot(ltri, g_scr[rows_c[c], :], lax.Precision.HIGHEST))
            gcum_t_c.append(gcum_c[c].T)
            bt_c.append(b_scr[rows_c[c], :])
        units = [(c, h) for c in range(GDN_GROUP) for h in range(HEADS)]
        hs = range(len(units))
        rows = [rows_c[c] for c, _ in units]
        cols = [slice(h * DH, (h + 1) * DH) for _, h in units]
        gc = [gcum_c[c][:, h:h + 1] for c, h in units]
        beta = [bt_c[c][:, HEADS + h:HEADS + h + 1] for c, h in units]
        decay = [jnp.where(causal, jnp.exp(jnp.where(causal, gc[u] - gcum_t_c[c][h:h + 1, :], 0.0)), 0.0)
                 for u, (c, h) in enumerate(units)]
        q = [qn_scr[rows[u], cols[u]] for u in hs]
        k = [kn_scr[rows[u], cols[u]] for u in hs]
        kb = [k[h].astype(BF16) for h in hs]
        kq = [_dot_nt(jnp.concatenate([kb[h], q[h].astype(BF16)], axis=0), kb[h]) for h in hs]
        nm = [jnp.where(strict, beta[h] * kq[h][:CHUNK] * decay[h], 0.0) for h in hs]
        pw = [jnp.where(base_mask, nm[h], 0.0) for h in hs]
        e = [-pw[h] for h in hs]
        for _ in range(INV_BASE_LEVELS):
            pwb = [pw[h].astype(BF16) for h in hs]
            pw = [_dot(pwb[h], pwb[h]) for h in hs]
            ep = [_dot(e[h].astype(BF16), pw[h].astype(BF16)) for h in hs]
            e = [e[h] + pw[h] + ep[h] for h in hs]
        for off_mask in off_masks:
            c = [jnp.where(off_mask, nm[h], 0.0) for h in hs]
            x = [c[h] + _dot(e[h].astype(BF16), c[h].astype(BF16)) for h in hs]
            xe = [_dot(x[h].astype(BF16), e[h].astype(BF16)) for h in hs]
            e = [e[h] - (x[h] + xe[h]) for h in hs]
        egc = [jnp.exp(gc[h]) for h in hs]
        rhs = [jnp.concatenate([vn_scr[rows[h], cols[h]] * beta[h], k[h] * (beta[h] * egc[h])], axis=1) for h in hs]
        uw = [rhs[h] + _dot(e[h].astype(BF16), rhs[h].astype(BF16)) for h in hs]
        wq = [jnp.concatenate([uw[h][:, DH:], q[h] * egc[h]], axis=0).astype(BF16) for h in hs]
        g_last = [gcum_c[c][CHUNK - 1:CHUNK, h:h + 1] for c, h in units]
        att_kd = [jnp.concatenate([kq[h][CHUNK:] * decay[h], (k[h] * jnp.exp(g_last[h] - gc[h])).T],
                                  axis=0).astype(BF16) for h in hs]
        for c in range(GDN_GROUP):
            us = [u for u in hs if units[u][0] == c]
            s = {u: s_scr[units[u][1]] for u in us}
            ws_qs = {u: _dot(wq[u], s[u].astype(BF16)) for u in us}
            vnb = {u: (uw[u][:, :DH] - ws_qs[u][:CHUNK]).astype(BF16) for u in us}
            av = {u: _dot(att_kd[u], vnb[u]) for u in us}
            for u in us:
                s_scr[units[u][1]] = s[u] * jnp.exp(g_last[u]) + av[u][CHUNK:]
                o = ws_qs[u][CHUNK:] + av[u][:CHUNK]
                o = o * lax.rsqrt(jnp.mean(o * o, axis=-1, keepdims=True) + NORM_EPS) * nw_ref[...]
                o = o * _silu(z_ref[rows[u], cols[u]])
                o_ref[rows[u], cols[u]] = o.astype(o_ref.dtype)
        return carry

    lax.fori_loop(0, nchunk // GDN_GROUP, chunk_group, 0)

    @pl.when(tb == pl.num_programs(1) - 1)
    def _():
        sout_ref[0] = s_scr[...]
        ctail_ref[0] = prev_scr[...]


def _gdn_prompt(p, o, bsz, seq, tt, cw, alog_row, dtb_row, nw):
    nchunk = tt // CHUNK
    assert nchunk % GDN_GROUP == 0
    nt = seq // tt
    n = p.shape[0]
    col = lambda c: pl.BlockSpec((tt, HW), lambda b, t, c=c: (b * nt + t, c))
    full = lambda a: pl.BlockSpec(a.shape, lambda b, t: (0,) * a.ndim)
    return pl.pallas_call(
        functools.partial(_gdn_body, nchunk=nchunk, tt=tt),
        out_shape=(jax.ShapeDtypeStruct((n, 2 * HW), BF16),
                   jax.ShapeDtypeStruct((bsz, HEADS, DH, DH), F32),
                   jax.ShapeDtypeStruct((bsz, 3, SUBLANES, HW), F32)),
        grid=(bsz, nt),
        in_specs=[col(COL_QD), col(COL_KD), col(COL_VD), col(COL_ZD),
                  pl.BlockSpec((tt, LANES), lambda b, t: (b * nt + t, TAIL_COL // LANES)),
                  pl.BlockSpec(memory_space=pl.ANY),
                  full(cw), full(alog_row), full(dtb_row), full(nw)],
        out_specs=(pl.BlockSpec((tt, HW), lambda b, t: (b * nt + t, 1)),
                   pl.BlockSpec((1, HEADS, DH, DH), lambda b, t: (b, 0, 0, 0)),
                   pl.BlockSpec((1, 3, SUBLANES, HW), lambda b, t: (b, 0, 0, 0))),
        scratch_shapes=[pltpu.VMEM((HEADS, DH, DH), F32),
                        pltpu.VMEM((3, SUBLANES, HW), F32),
                        pltpu.VMEM((SUBLANES + tt, HW), F32),
                        pltpu.VMEM((tt, HW), F32), pltpu.VMEM((tt, HW), F32), pltpu.VMEM((tt, HW), F32),
                        pltpu.VMEM((tt, LANES), F32), pltpu.VMEM((tt, LANES), F32)],
        input_output_aliases={5: 0},
        compiler_params=_cparams(("arbitrary", "arbitrary")),
        name="gdn_prompt",
    )(p, p, p, p, p, o, cw, alog_row, dtb_row, nw)


def _to_col(row, eye_mask):
    return jnp.sum(jnp.where(eye_mask, row, 0.0), axis=1, keepdims=True)


def _dec_body(qr_ref, kr_ref, vr_ref, gr_ref, qd_ref, kd_ref, vd_ref, zd_ref, tail_ref,
              sret_ref, sgdn_ref, sconv_ref, o_in_ref, cos_ref, sin_ref, gam_ref, cw_ref, alog_ref, dtb_ref,
              gnw_ref, gnb_ref, nw_ref,
              o_ref, sret_o, sgdn_o, conv_o,
              qr_s, kr_s, qd_s, kd_s, vd_s, gb_s, o_scr, *, bb):
    del o_in_ref
    step = pl.program_id(0)

    @pl.when(step == 0)
    def _():
        cos2 = cos_ref[...]
        sin2 = sin_ref[...]
        for h in range(HEADS):
            cs = slice(h * DH, (h + 1) * DH)
            qr_s[:, cs] = _rope(qr_ref[:, cs], cos2, sin2)
            kr_s[:, cs] = _rope(kr_ref[:, cs], cos2, sin2) * (DH ** -0.5)
        for gi, (src, dst) in enumerate(((qd_ref, qd_s), (kd_ref, kd_s), (vd_ref, vd_s))):
            gcols = slice(gi * HW, (gi + 1) * HW)
            x = src[...]
            acc = x * cw_ref[CONV_W - 1:CONV_W, gcols]
            for i in range(CONV_W - 1):
                acc = acc + sconv_ref[i, :, gcols] * cw_ref[i:i + 1, gcols]
            for i in range(CONV_W - 2):
                conv_o[i, :, gcols] = sconv_ref[i + 1, :, gcols]
            conv_o[CONV_W - 2, :, gcols] = x
            y = _silu(acc)
            for h in range(HEADS):
                cs = slice(h * DH, (h + 1) * DH)
                seg = y[:, cs]
                if gi < 2:
                    seg = seg * lax.rsqrt(jnp.sum(seg * seg, axis=-1, keepdims=True) + L2_EPS)
                if gi == 0:
                    seg = seg * (DH ** -0.5)
                dst[:, cs] = seg
        tail = tail_ref[...]
        lane = lax.broadcasted_iota(I32, tail.shape, 1)
        g = -jnp.exp(alog_ref[...]) * _softplus(tail + dtb_ref[...])
        gb_s[...] = jnp.where(lane < HEADS, g, jax.nn.sigmoid(tail))

    eye_mask = lax.broadcasted_iota(I32, (DH, DH), 0) == lax.broadcasted_iota(I32, (DH, DH), 1)

    rows8 = pl.ds(pl.multiple_of(step * bb, bb), bb)
    sub = lax.broadcasted_iota(I32, (bb, DH), 0)

    def per_seq(j, o8):
        pick = lambda ref, cs: jnp.sum(jnp.where(sub == j, ref[rows8, cs], 0.0), axis=0, keepdims=True)
        gb = pick(gb_s, slice(0, LANES))
        pieces_r, pieces_d = [], []
        for h in range(HEADS):
            cs = slice(h * DH, (h + 1) * DH)
            s = sret_ref[j, h]
            kcol = _to_col(pick(kr_s, cs), eye_mask)
            qcol = _to_col(pick(qr_s, cs), eye_mask)
            sn = s * gam_ref[h] + kcol * pick(vr_ref, cs)
            sret_o[j, h] = sn
            o = jnp.sum(qcol * sn, axis=0, keepdims=True)
            mu = jnp.mean(o, axis=-1, keepdims=True)
            d = o - mu
            var = jnp.mean(d * d, axis=-1, keepdims=True)
            on = d * lax.rsqrt(var + LN_EPS)
            pieces_r.append(_silu(pick(gr_ref, cs)) * (on * gnw_ref[:, cs] + gnb_ref[:, cs]))
            s = sgdn_ref[j, h]
            eg = jnp.exp(gb[:, h:h + 1])
            beta = gb[:, HEADS + h:HEADS + h + 1]
            kcol = _to_col(pick(kd_s, cs), eye_mask)
            qcol = _to_col(pick(qd_s, cs), eye_mask)
            ks = jnp.sum(kcol * s, axis=0, keepdims=True)
            v_new = beta * (pick(vd_s, cs) - eg * ks)
            sn = s * eg + kcol * v_new
            sgdn_o[j, h] = sn
            o = jnp.sum(qcol * sn, axis=0, keepdims=True)
            o = o * lax.rsqrt(jnp.mean(o * o, axis=-1, keepdims=True) + NORM_EPS) * nw_ref[...]
            pieces_d.append(o * _silu(pick(zd_ref, cs)))
        orow = jnp.concatenate(pieces_r + pieces_d, axis=1)
        return jnp.where(lax.broadcasted_iota(I32, o8.shape, 0) == j, orow, o8)

    o_scr[rows8, :] = lax.fori_loop(0, bb, per_seq, jnp.zeros((bb, 2 * HW), F32))

    @pl.when(step == pl.num_programs(0) - 1)
    def _():
        o_ref[...] = o_scr[...].astype(o_ref.dtype)


def _mixer_sample(p, o, n_prompt, dbs, s_ret, s_gdn, layer, s_conv_t, offset, cw, alog_row, dtb_row, gnw, gnb, nw):
    bb = SUBLANES
    n = p.shape[0]
    rb = n_prompt // dbs
    cos2, sin2 = _rope_tables(1, offset)
    log_gamma = jnp.log(1.0 - 2.0 ** (-5.0 - jnp.arange(HEADS, dtype=F32)))
    gam = jnp.broadcast_to(jnp.exp(log_gamma)[:, None, None], (HEADS, 1, LANES)).astype(F32)
    col = lambda c: pl.BlockSpec((dbs, HW), lambda s, c=c: (rb, c))
    full = lambda a: pl.BlockSpec(a.shape, lambda s: (0,) * a.ndim)
    st = pl.BlockSpec((bb, HEADS, DH, DH), lambda s: (s, 0, 0, 0))
    st_in = pl.BlockSpec((None, bb, HEADS, DH, DH), lambda s: (layer, s, 0, 0, 0))
    vec = lambda: pltpu.VMEM((dbs, HW), F32)
    return pl.pallas_call(
        functools.partial(_dec_body, bb=bb),
        out_shape=(jax.ShapeDtypeStruct((n, 2 * HW), BF16),
                   jax.ShapeDtypeStruct(s_ret.shape[1:], F32),
                   jax.ShapeDtypeStruct(s_gdn.shape[1:], F32),
                   jax.ShapeDtypeStruct(s_conv_t.shape, F32)),
        grid=(dbs // bb,),
        in_specs=[col(COL_QR), col(COL_KR), col(COL_VR), col(COL_GR),
                  col(COL_QD), col(COL_KD), col(COL_VD), col(COL_ZD),
                  pl.BlockSpec((dbs, LANES), lambda s: (rb, TAIL_COL // LANES)),
                  st_in, st_in, full(s_conv_t),
                  pl.BlockSpec(memory_space=pl.ANY),
                  full(cos2), full(sin2), full(gam), full(cw), full(alog_row), full(dtb_row),
                  full(gnw), full(gnb), full(nw)],
        out_specs=(pl.BlockSpec((dbs, 2 * HW), lambda s: (rb, 0)), st, st, full(s_conv_t)),
        scratch_shapes=[vec(), vec(), vec(), vec(), vec(),
                        pltpu.VMEM((dbs, LANES), F32), pltpu.VMEM((dbs, 2 * HW), F32)],
        input_output_aliases={12: 0},
        compiler_params=_cparams(("arbitrary",)),
        name="mixer_sample",
    )(p, p, p, p, p, p, p, p, p, s_ret, s_gdn, s_conv_t, o, cos2, sin2, gam, cw, alog_row, dtb_row,
      gnw, gnb, nw)


def _post_mixer_body(o_ref, x_ref, wout_ref, g_ref, b_ref, wr_ref, br_ref,
                     x1_ref, x1t_ref, e4_ref, r4_ref, g4_ref, cnt_ref, carry_scr, *, alpha, tm):
    i = pl.program_id(0)

    @pl.when(i == 0)
    def _():
        carry_scr[...] = jnp.zeros_like(carry_scr)

    y = _dot(o_ref[...], wout_ref[...])
    x1 = _layer_norm(alpha * x_ref[...] + y, g_ref[...], b_ref[...])
    x1_ref[...] = x1
    _rows_to_tiles(x1t_ref, x1)
    logits = _dot(x1.astype(BF16), wr_ref[...]) + br_ref[...]
    lane = lax.broadcasted_iota(I32, (tm, LANES), 1)
    lane_f = lane.astype(F32)
    work = logits
    ohs, vals, idxs = [], [], []
    for _ in range(TOP_K):
        m = jnp.max(work, axis=1, keepdims=True)
        idx = jnp.min(jnp.where(work == m, lane_f, float(LANES)), axis=1, keepdims=True)
        oh = lane_f == idx
        ohs.append(oh)
        vals.append(m)
        idxs.append(idx)
        work = jnp.where(oh, NEG, work)
    es = [jnp.exp(v - vals[0]) for v in vals]
    den = es[0] + es[1] + es[2] + es[3]
    sel = jnp.zeros((tm, LANES), F32)
    for oh in ohs:
        sel = sel + jnp.where(oh, 1.0, 0.0)
    r_i = lax.broadcasted_iota(I32, (tm, tm), 0)
    c_i = lax.broadcasted_iota(I32, (tm, tm), 1)
    ltri = jnp.where(r_i >= c_i, 1.0, 0.0).astype(BF16)
    incl = _dot(ltri, sel.astype(BF16))
    excl = incl - sel + carry_scr[0:1, :]
    carry_scr[...] = carry_scr[...] + incl[tm - 1:tm, :]
    e4 = jnp.zeros((tm, LANES), F32)
    r4 = jnp.zeros((tm, LANES), F32)
    g4 = jnp.zeros((tm, LANES), F32)
    for k in range(TOP_K):
        rk = jnp.sum(jnp.where(ohs[k], excl, 0.0), axis=1, keepdims=True)
        e4 = jnp.where(lane == k, idxs[k], e4)
        r4 = jnp.where(lane == k, rk, r4)
        g4 = jnp.where(lane == k, es[k] / den, g4)
    e4_ref[...] = e4.astype(I32)
    r4_ref[...] = r4.astype(I32)
    g4_ref[...] = g4
    cnt_ref[...] = carry_scr[...]


def _post_mixer(o, x, wout_bf, ln_g, ln_b, wr_bf, br_row, alpha, tm):
    n, d = x.shape
    row = lambda w: pl.BlockSpec((tm, w), lambda i: (i, 0))
    full = lambda a: pl.BlockSpec(a.shape, lambda i: (0,) * a.ndim)
    return pl.pallas_call(
        functools.partial(_post_mixer_body, alpha=alpha, tm=tm),
        out_shape=(jax.ShapeDtypeStruct((n, d), F32),
                   jax.ShapeDtypeStruct((n * SUBLANES, LANES), F32),
                   jax.ShapeDtypeStruct((n, LANES), I32),
                   jax.ShapeDtypeStruct((n, LANES), I32),
                   jax.ShapeDtypeStruct((n, LANES), F32),
                   jax.ShapeDtypeStruct((SUBLANES, LANES), F32)),
        grid=(n // tm,),
        in_specs=[row(2 * HW), row(d), full(wout_bf), full(ln_g), full(ln_b), full(wr_bf), full(br_row)],
        out_specs=(row(d), pl.BlockSpec((tm * SUBLANES, LANES), lambda i: (i, 0)),
                   row(LANES), row(LANES), row(LANES),
                   pl.BlockSpec((SUBLANES, LANES), lambda i: (0, 0))),
        scratch_shapes=[pltpu.VMEM((SUBLANES, LANES), F32)],
        compiler_params=_cparams(("arbitrary",)),
        name="post_mixer",
    )(o, x, wout_bf, ln_g, ln_b, wr_bf, br_row)


def _plan_body(e4_ref, r4_ref, cnt_ref, d4_ref, blk_ref, seg_ref, pstart_scr, *, tm, n_experts, rows):
    nbp = blk_ref.shape[0]

    @pl.when(pl.program_id(0) == 0)
    def _():
        cnt = cnt_ref[...]
        padded = jnp.floor((cnt + (MOE_BM - 1.0)) * (1.0 / MOE_BM)) * MOE_BM
        m_i = lax.broadcasted_iota(I32, (LANES, LANES), 0)
        j_i = lax.broadcasted_iota(I32, (LANES, LANES), 1)
        upper = jnp.where(m_i <= j_i, 1.0, 0.0).astype(F32)
        pad_end = _dot(padded, upper, lax.Precision.HIGHEST)
        pstart_scr[...] = pad_end - padded
        bstart = lax.broadcasted_iota(I32, (nbp, LANES), 0).astype(F32) * MOE_BM
        blane = lax.broadcasted_iota(I32, (nbp, LANES), 1)
        hit = jnp.where((pad_end[0:1, :] <= bstart) & (blane < n_experts), 1.0, 0.0)
        be = jnp.minimum(jnp.sum(hit, axis=1, keepdims=True), n_experts - 1.0)
        blk_ref[...] = jnp.broadcast_to(be, (nbp, LANES)).astype(I32)
        srow = lax.broadcasted_iota(I32, (SUBLANES, LANES), 0)
        slane = lax.broadcasted_iota(I32, (SUBLANES, LANES), 1)
        total = jnp.sum(jnp.where(slane == n_experts - 1, pad_end, 0.0), axis=1, keepdims=True)
        seg_end = jnp.where(slane == n_experts - 1, float(rows), pad_end)
        seg = jnp.where(srow == 0, pad_end - padded + cnt,
                        jnp.where(srow == 1, seg_end,
                                  jnp.where(srow == 2, total * (1.0 / MOE_BM), pad_end * (1.0 / MOE_BM))))
        seg_ref[...] = seg.astype(I32)

    pad_start = pstart_scr[0:1, :]
    lane = lax.broadcasted_iota(I32, (tm, LANES), 1)
    e4 = e4_ref[...]
    r4 = r4_ref[...].astype(F32)
    d4 = jnp.zeros((tm, LANES), F32)
    for k in range(TOP_K):
        ek = e4[:, k:k + 1]
        ps = jnp.sum(jnp.where(lane == ek, pad_start, 0.0), axis=1, keepdims=True)
        d4 = jnp.where(lane == k, ps + r4[:, k:k + 1], d4)
    d4_ref[...] = d4.astype(I32)


def _plan(e4, r4, cnt, tm, n_experts, rows):
    n = e4.shape[0]
    nbp = pl.cdiv(rows // MOE_BM, SUBLANES) * SUBLANES
    row = pl.BlockSpec((tm, LANES), lambda i: (i, 0))
    return pl.pallas_call(
        functools.partial(_plan_body, tm=tm, n_experts=n_experts, rows=rows),
        out_shape=(jax.ShapeDtypeStruct((n, LANES), I32),
                   jax.ShapeDtypeStruct((nbp, LANES), I32),
                   jax.ShapeDtypeStruct((SUBLANES, LANES), I32)),
        grid=(n // tm,),
        in_specs=[row, row, pl.BlockSpec((SUBLANES, LANES), lambda i: (0, 0))],
        out_specs=(row, pl.BlockSpec((nbp, LANES), lambda i: (0, 0)),
                   pl.BlockSpec((SUBLANES, LANES), lambda i: (0, 0))),
        scratch_shapes=[pltpu.VMEM((SUBLANES, LANES), F32)],
        compiler_params=_cparams(("arbitrary",)),
        name="moe_plan",
    )(e4, r4, cnt)


def _tile_at(ref, r, lead=()):
    return ref.at[lead + (pl.ds(pl.multiple_of(r * SUBLANES, SUBLANES), SUBLANES), slice(None))]


def _issue(copy_of, count):
    def start(i, carry):
        for u in range(DMA_UNROLL):
            copy_of(i * DMA_UNROLL + u).start(priority=u % 2)
        return carry

    lax.fori_loop(0, count // DMA_UNROLL, start, 0)


def _wait(copy_of, count):
    def wait(pidx, carry):
        copy_of(pidx).wait()
        return carry

    lax.fori_loop(0, count, wait, 0, unroll=DMA_UNROLL)


def _dispatch_body(zs_ref, ze_ref, dest_ref, x_ref, xs_ref, zero_scr, sem, zsem, *, n_experts):
    def token_copy(pidx):
        t = lax.shift_right_logical(pidx, 2)
        return pltpu.make_async_copy(_tile_at(x_ref, t), _tile_at(xs_ref, dest_ref[0, 0, pidx]), sem)

    _issue(token_copy, TOK_TILE * TOP_K)
    _wait(token_copy, TOK_TILE * TOP_K)

    @pl.when(pl.program_id(0) == pl.num_programs(0) - 1)
    def _():
        zero_scr[...] = jnp.zeros_like(zero_scr)

        def zero_rows(first, nrows, wait):
            cp = pltpu.make_async_copy(zero_scr.at[pl.ds(0, nrows * SUBLANES), :],
                                       xs_ref.at[pl.ds(pl.multiple_of(first * SUBLANES, SUBLANES),
                                                       nrows * SUBLANES), :], zsem)
            cp.wait() if wait else cp.start()

        def per_expert(wait):
            def body(e, carry):
                first = zs_ref[e]
                count = ze_ref[e] - first
                nbig = lax.shift_right_logical(count, ZERO_ROWS.bit_length() - 1)

                def big(j, c):
                    zero_rows(first + j * ZERO_ROWS, ZERO_ROWS, wait)
                    return c

                lax.fori_loop(0, nbig, big, 0)
                cur = first + nbig * ZERO_ROWS
                piece = ZERO_ROWS // 2
                while piece >= 1:
                    has = lax.bitwise_and(count, piece)

                    @pl.when(has != 0)
                    def _(cur=cur, piece=piece):
                        zero_rows(cur, piece, wait)

                    cur = cur + has
                    piece //= 2
                return carry
            lax.fori_loop(0, n_experts, body, 0)

        per_expert(False)
        per_expert(True)


def _dispatch(x1t, dest3, zstart, zend, rows, n_experts):
    n = x1t.shape[0] // SUBLANES
    npairs = TOK_TILE * TOP_K
    grid_spec = pltpu.PrefetchScalarGridSpec(
        num_scalar_prefetch=2,
        grid=(n // TOK_TILE,),
        in_specs=[pl.BlockSpec((1, 1, npairs), lambda i, zs, ze: (i, 0, 0), memory_space=pltpu.SMEM),
                  pl.BlockSpec((TOK_TILE * SUBLANES, LANES), lambda i, zs, ze: (i, 0))],
        out_specs=pl.BlockSpec(memory_space=pl.ANY),
        scratch_shapes=[pltpu.VMEM((ZERO_ROWS * SUBLANES, LANES), F32), pltpu.SemaphoreType.DMA(()),
                        pltpu.SemaphoreType.DMA(())],
    )
    return pl.pallas_call(
        functools.partial(_dispatch_body, n_experts=n_experts),
        out_shape=jax.ShapeDtypeStruct((rows * SUBLANES, LANES), F32),
        grid_spec=grid_spec,
        compiler_params=_cparams(("arbitrary",)),
        name="moe_dispatch",
    )(zstart, zend, dest3, x1t)


def _ffn_body(blk_ref, nused_ref, segend_ref, xs_ref, bgu_ref, bd_ref, wgu_hbm, wd_hbm, ys_ref,
              wgu_f32, wd_f32, wgu_bf, wd_bf, sems, slot_ref, *, de, layer):
    i = pl.program_id(0)
    nused = nused_ref[0]
    e = blk_ref[i]
    prev = blk_ref[jnp.maximum(i - 1, 0)]

    def weight_copies(expert, slot):
        return (pltpu.make_async_copy(wgu_hbm.at[layer, expert], wgu_f32.at[slot], sems.at[0, slot]),
                pltpu.make_async_copy(wd_hbm.at[layer, expert], wd_f32.at[slot], sems.at[1, slot]))

    @pl.when((i == 0) & (nused > 0))
    def _():
        slot_ref[0] = 0
        for c in weight_copies(e, 0):
            c.start()

    @pl.when((i < nused) & ((i == 0) | (e != prev)))
    def _():
        slot = slot_ref[0]
        for c in weight_copies(e, slot):
            c.wait()
        nxt = segend_ref[e]

        @pl.when(nxt < nused)
        def _():
            for c in weight_copies(blk_ref[nxt], 1 - slot):
                c.start()

        rb = 128
        for r in range(0, wgu_bf.shape[0], rb):
            wgu_bf[r:r + rb, :] = wgu_f32[slot, r:r + rb, :].astype(BF16)
        for r in range(0, wd_bf.shape[0], rb):
            wd_bf[r:r + rb, :] = wd_f32[slot, r:r + rb, :].astype(BF16)
        slot_ref[0] = 1 - slot

    @pl.when(i < nused)
    def _():
        xb = _tiles_to_rows(xs_ref, MOE_BM).astype(BF16)
        gu = _dot(xb, wgu_bf[...]) + bgu_ref[...]
        gate = jnp.minimum(gu[:, :de], SWIGLU_LIMIT)
        up = jnp.clip(gu[:, de:], -SWIGLU_LIMIT, SWIGLU_LIMIT)
        act = (up + 1.0) * gate * jax.nn.sigmoid(SWIGLU_ALPHA * gate)
        _rows_to_tiles(ys_ref, _dot(act.astype(BF16), wd_bf[...]) + bd_ref[...])

    @pl.when(i >= nused)
    def _():
        ys_ref[...] = jnp.zeros_like(ys_ref)


def _ffn(xs, blk_e, nused, segend_blk, w_gu, b_gu, w_down, b_down, layer):
    rows = xs.shape[0] // SUBLANES
    depth, n_experts, d, de2 = w_gu.shape
    de = de2 // 2
    nb = rows // MOE_BM
    tile_blk = pl.BlockSpec((MOE_BM * SUBLANES, LANES), lambda i, blk, nu, se: (i, 0))
    grid_spec = pltpu.PrefetchScalarGridSpec(
        num_scalar_prefetch=3,
        grid=(nb,),
        in_specs=[tile_blk,
                  pl.BlockSpec((None, None, 1, de2), lambda i, blk, nu, se: (layer, blk[i], 0, 0)),
                  pl.BlockSpec((None, None, 1, d), lambda i, blk, nu, se: (layer, blk[i], 0, 0)),
                  pl.BlockSpec(memory_space=pl.ANY),
                  pl.BlockSpec(memory_space=pl.ANY)],
        out_specs=tile_blk,
        scratch_shapes=[pltpu.VMEM((2, d, de2), F32), pltpu.VMEM((2, de, d), F32),
                        pltpu.VMEM((d, de2), BF16), pltpu.VMEM((de, d), BF16),
                        pltpu.SemaphoreType.DMA((2, 2)), pltpu.SMEM((1,), I32)],
    )
    return pl.pallas_call(
        functools.partial(_ffn_body, de=de, layer=layer),
        out_shape=jax.ShapeDtypeStruct((rows * SUBLANES, LANES), F32),
        grid_spec=grid_spec,
        compiler_params=_cparams(("arbitrary",)),
        name="moe_ffn",
    )(blk_e, nused, segend_blk, xs, b_gu.reshape(depth, n_experts, 1, de2),
      b_down.reshape(depth, n_experts, 1, d), w_gu, w_down)


def _combine_body(dest_ref, dest_next_ref, g4_ref, x1_ref, g_ref, b_ref, ys_ref, *rest, alpha, split_tiles):
    *out_refs, buf, sems = rest
    i = pl.program_id(0)
    slot = lax.rem(i, 2)
    npairs = TOK_TILE * TOP_K

    def gather(dref, s):
        def row_copy(pidx):
            t = lax.shift_right_logical(pidx, 2)
            k = lax.bitwise_and(pidx, TOP_K - 1)
            return pltpu.make_async_copy(_tile_at(ys_ref, dref[0, 0, pidx]), _tile_at(buf, t, (s, k)), sems.at[s])
        return row_copy

    @pl.when(i == 0)
    def _():
        _issue(gather(dest_ref, 0), npairs)

    @pl.when(i + 1 < pl.num_programs(0))
    def _():
        _issue(gather(dest_next_ref, 1 - slot), npairs)

    _wait(gather(dest_ref, slot), npairs)
    g4 = g4_ref[...]
    f = g4[:, 0:1] * _tiles_to_rows(buf, TOK_TILE, (slot, 0))
    for k in range(1, TOP_K):
        f = f + g4[:, k:k + 1] * _tiles_to_rows(buf, TOK_TILE, (slot, k))
    res = _layer_norm(alpha * x1_ref[...] + f, g_ref[...], b_ref[...])
    if split_tiles is None:
        out_refs[0][...] = res
    else:

        @pl.when(i < split_tiles)
        def _():
            out_refs[0][...] = res

        @pl.when(i >= split_tiles)
        def _():
            out_refs[1][...] = res


def _combine(ys, dest3, g4, x1, ln_g, ln_b, alpha, n_prompt=None):
    n, d = x1.shape
    full = lambda a: pl.BlockSpec(a.shape, lambda i: (0,) * a.ndim)
    tile = lambda f: pl.BlockSpec((TOK_TILE, d), f)
    if n_prompt is None:
        split_tiles = None
        out_shape = jax.ShapeDtypeStruct((n, d), F32)
        out_specs = tile(lambda i: (i, 0))
    else:
        assert n - n_prompt == TOK_TILE and n_prompt % TOK_TILE == 0
        split_tiles = n_prompt // TOK_TILE
        out_shape = (jax.ShapeDtypeStruct((n_prompt, d), F32), jax.ShapeDtypeStruct((TOK_TILE, d), F32))
        out_specs = (tile(lambda i: (jnp.minimum(i, split_tiles - 1), 0)), tile(lambda i: (0, 0)))
    nsteps = n // TOK_TILE
    dest_spec = lambda f: pl.BlockSpec((1, 1, TOK_TILE * TOP_K), f, memory_space=pltpu.SMEM)
    return pl.pallas_call(
        functools.partial(_combine_body, alpha=alpha, split_tiles=split_tiles),
        out_shape=out_shape,
        grid=(nsteps,),
        in_specs=[dest_spec(lambda i: (i, 0, 0)),
                  dest_spec(lambda i: (jnp.minimum(i + 1, nsteps - 1), 0, 0)),
                  pl.BlockSpec((TOK_TILE, LANES), lambda i: (i, 0)),
                  tile(lambda i: (i, 0)),
                  full(ln_g), full(ln_b),
                  pl.BlockSpec(memory_space=pl.ANY)],
        out_specs=out_specs,
        scratch_shapes=[pltpu.VMEM((2, TOP_K, TOK_TILE * SUBLANES, LANES), F32), pltpu.SemaphoreType.DMA((2,))],
        compiler_params=_cparams(("arbitrary",)),
        name="moe_combine",
    )(dest3, dest3, g4, x1, ln_g, ln_b, ys)


def kernel(x_prompt, x_sample, state_ret, state_gdn, state_conv, w_in, conv_w, a_log, dt_bias, ret_gn_w, ret_gn_b, gdn_norm_w, w_out, ln1_g, ln1_b, w_router, b_router, w_gu, b_gu, w_down, b_down, ln2_g, ln2_b):
    bsz, seq, d = x_prompt.shape
    dbs, dseq, _ = x_sample.shape
    depth = w_in.shape[0]
    n_experts = w_router.shape[-1]
    assert dseq == 1 and dbs == LANES and seq % CHUNK == 0 and d == 2 * HW == SUBLANES * LANES
    assert 2 ** (INV_BASE_LEVELS + 1) == INV_BASE and CHUNK % INV_BASE == 0
    n_prompt = bsz * seq
    n = n_prompt + dbs
    alpha = (2.0 * depth) ** 0.25
    tm = _pick_tile(n, (384, 256, 128))
    tt = _pick_tile(seq, (512, 256, 128))
    rows = (pl.cdiv(n * TOP_K, MOE_BM) + n_experts) * MOE_BM

    x = jnp.concatenate([x_prompt.reshape(n_prompt, d), x_sample.reshape(dbs, d)], axis=0)
    row2 = lambda a: a.reshape(1, -1).astype(F32)
    pad_lanes = lambda a, fill: jnp.concatenate(
        [a.astype(F32), jnp.full((LANES - a.shape[0],), fill, F32)]).reshape(1, LANES)

    ret_p, gdn_p, conv_p, ret_s, gdn_s, conv_s = [], [], [], [], [], []
    for l in range(depth):
        w_in_bf = jnp.pad(w_in[l], ((0, 0), (0, IN_PAD - w_in.shape[-1]))).astype(BF16)
        p = _inproj(x, w_in_bf, tm)
        gnw, gnb, nw = row2(ret_gn_w[l]), row2(ret_gn_b[l]), row2(gdn_norm_w[l])
        alog_row, dtb_row = pad_lanes(a_log[l], 0.0), pad_lanes(dt_bias[l], 0.0)
        cw = conv_w[l].astype(F32)

        o, sr = _retention_prompt(p, n, bsz, seq, tt, gnw, gnb)
        o, sg, ctail = _gdn_prompt(p, o, bsz, seq, tt, cw, alog_row, dtb_row, nw)
        sconv_t = jnp.transpose(state_conv[l], (1, 0, 2))
        o, sr_s, sg_s, sc_s = _mixer_sample(p, o, n_prompt, dbs, state_ret, state_gdn, l, sconv_t,
                                            float(PAST_LEN), cw, alog_row, dtb_row, gnw, gnb, nw)
        ret_p.append(sr)
        gdn_p.append(sg)
        conv_p.append(jnp.transpose(ctail[:, :, SUBLANES - (CONV_W - 1):, :], (0, 2, 1, 3))
                      .reshape(bsz, CONV_W - 1, 3 * HW))
        ret_s.append(sr_s)
        gdn_s.append(sg_s)
        conv_s.append(jnp.transpose(sc_s, (1, 0, 2)))

        wr_bf = jnp.pad(w_router[l], ((0, 0), (0, LANES - n_experts))).astype(BF16)
        br_row = pad_lanes(b_router[l], NEG)
        x1, x1t, e4, r4, g4, cnt = _post_mixer(o, x, w_out[l].astype(BF16), row2(ln1_g[l]), row2(ln1_b[l]),
                                               wr_bf, br_row, alpha, tm)
        d4, blk, seg = _plan(e4, r4, cnt, _largest_tile(n, PLAN_TILE_CAP), n_experts, rows)
        dest3 = d4[:, :TOP_K].reshape(n // TOK_TILE, 1, TOK_TILE * TOP_K)
        xs = _dispatch(x1t, dest3, seg[0, :n_experts], seg[1, :n_experts], rows, n_experts)
        ys = _ffn(xs, blk[:rows // MOE_BM, 0], seg[2, :1], seg[3, :n_experts], w_gu, b_gu, w_down, b_down, l)
        if l + 1 < depth:
            x = _combine(ys, dest3, g4, x1, row2(ln2_g[l]), row2(ln2_b[l]), alpha)
        else:
            y_prompt, y_sample = _combine(ys, dest3, g4, x1, row2(ln2_g[l]), row2(ln2_b[l]), alpha, n_prompt)

    return (y_prompt.reshape(bsz, seq, d), y_sample.reshape(dbs, dseq, d),
            jnp.stack(ret_p), jnp.stack(gdn_p), jnp.stack(conv_p),
            jnp.stack(ret_s), jnp.stack(gdn_s), jnp.stack(conv_s))
```

```python
import functools
import math

import jax
import jax.numpy as jnp
from jax import lax
from jax.experimental import pallas as pl
from jax.experimental.pallas import tpu as pltpu

F32 = jnp.float32
BF16 = jnp.bfloat16
I32 = jnp.int32

HEADS = 4
DH = 128
HW = HEADS * DH
CONV_W = 4
ROPE_BASE = 10000.0
PAST_LEN = 16384
TOP_K = 4
SWIGLU_LIMIT = 7.0
SWIGLU_ALPHA = 1.702
LN_EPS = 1e-5
NORM_EPS = 1e-6
L2_EPS = 1e-6

LANES = 128
SUBLANES = 8
VMEM_LIMIT_BYTES = 56 * 1024 * 1024

CHUNK = 128
GDN_GROUP = 4
INV_BASE = 16
INV_BASE_LEVELS = 3
MOE_BM = 256
TOK_TILE = 128
ZERO_ROWS = 128
PLAN_TILE_CAP = 8192
DMA_UNROLL = 8
NEG = -3.0e38

COL_QR, COL_KR, COL_VR, COL_GR, COL_QD, COL_KD, COL_VD, COL_ZD = range(8)
TAIL_COL = 8 * HW
IN_PAD = TAIL_COL + LANES

_NT = (((1,), (1,)), ((), ()))


def _cparams(sem, vmem=VMEM_LIMIT_BYTES):
    return pltpu.CompilerParams(dimension_semantics=sem, vmem_limit_bytes=vmem)


def _dot(a, b, precision=None):
    return jnp.dot(a, b, preferred_element_type=F32, precision=precision)


def _dot_nt(a, b):
    return lax.dot_general(a, b, _NT, preferred_element_type=F32)


def _silu(x):
    return x * jax.nn.sigmoid(x)


def _layer_norm(h, g, b):
    mu = jnp.mean(h, axis=-1, keepdims=True)
    d = h - mu
    var = jnp.mean(d * d, axis=-1, keepdims=True)
    return d * lax.rsqrt(var + LN_EPS) * g + b


def _rope(x, cos2, sin2):
    return x * cos2 + pltpu.roll(x, DH // 2, 1) * sin2


def _tiles_to_rows(ref, nrows, lead=()):
    return jnp.concatenate([ref[lead + (pl.ds(s, nrows, stride=SUBLANES), slice(None))] for s in range(SUBLANES)],
                           axis=1)


def _rows_to_tiles(ref, val):
    nrows = val.shape[0]
    for s in range(SUBLANES):
        ref[pl.ds(s, nrows, stride=SUBLANES), :] = val[:, s * LANES:(s + 1) * LANES]


def _pick_tile(n, candidates):
    for c in candidates:
        if n % c == 0:
            return c
    raise ValueError(f"no tile in {candidates} divides {n}")


def _largest_tile(n, cap):
    for parts in range(1, n + 1):
        if n % parts == 0 and (n // parts) % SUBLANES == 0 and n // parts <= cap:
            return n // parts
    raise ValueError(f"no tile of at most {cap} rows divides {n}")


def _inproj_body(x_ref, w_ref, o_ref):
    o_ref[...] = _dot(x_ref[...].astype(BF16), w_ref[...])


def _inproj(x, w_bf, tm):
    n, d = x.shape
    return pl.pallas_call(
        _inproj_body,
        out_shape=jax.ShapeDtypeStruct((n, IN_PAD), F32),
        grid=(n // tm,),
        in_specs=[pl.BlockSpec((tm, d), lambda i: (i, 0)),
                  pl.BlockSpec((d, IN_PAD), lambda i: (0, 0))],
        out_specs=pl.BlockSpec((tm, IN_PAD), lambda i: (i, 0)),
        compiler_params=_cparams(("arbitrary",)),
        name="inproj",
    )(x, w_bf)


def _ret_body(q_ref, k_ref, v_ref, g_ref, o_in_ref, cos_ref, sin_ref, intra_ref, qdec_ref, kdec_ref, cdec_ref,
              gnw_ref, gnb_ref, o_ref, sout_ref, s_scr, *, nchunk):
    del o_in_ref
    tb = pl.program_id(1)

    @pl.when(tb == 0)
    def _():
        s_scr[...] = jnp.zeros_like(s_scr)

    def chunk(ci, carry):
        rows = pl.ds(pl.multiple_of(ci * CHUNK, CHUNK), CHUNK)
        cos2 = cos_ref[rows, :]
        sin2 = sin_ref[rows, :]
        hs = range(HEADS)
        cols = [slice(h * DH, (h + 1) * DH) for h in hs]
        q = [_rope(q_ref[rows, cols[h]], cos2, sin2) for h in hs]
        k = [_rope(k_ref[rows, cols[h]], cos2, sin2) * (DH ** -0.5) for h in hs]
        vb = [v_ref[rows, cols[h]].astype(BF16) for h in hs]
        s = [s_scr[h] for h in hs]
        att = [_dot_nt(q[h].astype(BF16), k[h].astype(BF16)) * intra_ref[h] for h in hs]
        qs = [_dot((q[h] * qdec_ref[h]).astype(BF16), s[h].astype(BF16)) for h in hs]
        av = [_dot(jnp.concatenate([att[h], (k[h] * kdec_ref[h]).T], axis=0).astype(BF16), vb[h]) for h in hs]
        for h in hs:
            s_scr[h] = s[h] * cdec_ref[h] + av[h][CHUNK:]
            o = av[h][:CHUNK] + qs[h]
            mu = jnp.mean(o, axis=-1, keepdims=True)
            d = o - mu
            var = jnp.mean(d * d, axis=-1, keepdims=True)
            on = d * lax.rsqrt(var + LN_EPS)
            res = _silu(g_ref[rows, cols[h]]) * (on * gnw_ref[:, cols[h]] + gnb_ref[:, cols[h]])
            o_ref[rows, cols[h]] = res.astype(o_ref.dtype)
        return carry

    lax.fori_loop(0, nchunk, chunk, 0)

    @pl.when(tb == pl.num_programs(1) - 1)
    def _():
        sout_ref[0] = s_scr[...]


def _ret_tables(c):
    log_gamma = jnp.log(1.0 - 2.0 ** (-5.0 - jnp.arange(HEADS, dtype=F32)))
    idx = jnp.arange(c, dtype=F32)
    diff = idx[:, None] - idx[None, :]
    causal = diff >= 0
    intra = jnp.where(causal, jnp.exp(log_gamma[:, None, None] * jnp.where(causal, diff, 0.0)), 0.0)
    q_dec = jnp.exp(log_gamma[:, None] * (idx + 1.0))[..., None]
    k_dec = jnp.exp(log_gamma[:, None] * (c - 1.0 - idx))[..., None]
    c_dec = jnp.exp(log_gamma * c)[:, None, None]
    bc = lambda a, r: jnp.broadcast_to(a, (HEADS, r, LANES)).astype(F32)
    return intra, bc(q_dec, c), bc(k_dec, c), bc(c_dec, 1)


def _rope_tables(t, offset):
    half = DH // 2
    inv_freq = ROPE_BASE ** (-jnp.arange(half, dtype=F32) / half)
    pos = jnp.arange(t, dtype=F32) + offset
    ang = pos[:, None] * inv_freq[None, :]
    cos, sin = jnp.cos(ang), jnp.sin(ang)
    return jnp.concatenate([cos, cos], -1), jnp.concatenate([-sin, sin], -1)


def _retention_prompt(p, n, bsz, seq, tt, gnw, gnb):
    nchunk = tt // CHUNK
    nt = seq // tt
    cos2, sin2 = _rope_tables(seq, 0)
    intra, qdec, kdec, cdec = _ret_tables(CHUNK)
    col = lambda c: pl.BlockSpec((tt, HW), lambda b, t, c=c: (b * nt + t, c))
    full = lambda a: pl.BlockSpec(a.shape, lambda b, t: (0,) * a.ndim)
    return pl.pallas_call(
        functools.partial(_ret_body, nchunk=nchunk),
        out_shape=(jax.ShapeDtypeStruct((n, 2 * HW), BF16),
                   jax.ShapeDtypeStruct((bsz, HEADS, DH, DH), F32)),
        grid=(bsz, nt),
        in_specs=[col(COL_QR), col(COL_KR), col(COL_VR), col(COL_GR),
                  pl.BlockSpec(memory_space=pl.ANY),
                  pl.BlockSpec((tt, DH), lambda b, t: (t, 0)),
                  pl.BlockSpec((tt, DH), lambda b, t: (t, 0)),
                  full(intra), full(qdec), full(kdec), full(cdec), full(gnw), full(gnb)],
        out_specs=(pl.BlockSpec((tt, HW), lambda b, t: (b * nt + t, 0)),
                   pl.BlockSpec((1, HEADS, DH, DH), lambda b, t: (b, 0, 0, 0))),
        scratch_shapes=[pltpu.VMEM((HEADS, DH, DH), F32)],
        input_output_aliases={4: 0},
        compiler_params=_cparams(("arbitrary", "arbitrary")),
        name="retention_prompt",
    )(p, p, p, p, jnp.zeros((n, 2 * HW), BF16), cos2, sin2, intra, qdec, kdec, cdec, gnw, gnb)


def _softplus(x):
    return jnp.maximum(x, 0.0) + jnp.log1p(jnp.exp(-jnp.abs(x)))


def _gdn_body(qd_ref, kd_ref, vd_ref, z_ref, tail_ref, o_in_ref, cw_ref, alog_ref, dtb_ref, nw_ref,
              o_ref, sout_ref, ctail_ref, s_scr, prev_scr, xe_scr, qn_scr, kn_scr, vn_scr, g_scr, b_scr,
              *, nchunk, tt, group):
    del o_in_ref
    tb = pl.program_id(1)

    @pl.when(tb == 0)
    def _():
        s_scr[...] = jnp.zeros_like(s_scr)
        prev_scr[...] = jnp.zeros_like(prev_scr)

    for gi, (src, dst) in enumerate(((qd_ref, qn_scr), (kd_ref, kn_scr), (vd_ref, vn_scr))):
        gcols = slice(gi * HW, (gi + 1) * HW)
        xe_scr[0:SUBLANES, :] = prev_scr[gi]
        xe_scr[SUBLANES:SUBLANES + tt, :] = src[...]
        prev_scr[gi] = src[tt - SUBLANES:tt, :]
        for c in range(nchunk):
            base = SUBLANES + c * CHUNK
            acc = xe_scr[base:base + CHUNK, :] * cw_ref[CONV_W - 1:CONV_W, gcols]
            for j in range(1, CONV_W):
                acc = acc + xe_scr[base - j:base - j + CHUNK, :] * cw_ref[CONV_W - 1 - j:CONV_W - j, gcols]
            y = _silu(acc)
            for h in range(HEADS):
                cs = slice(h * DH, (h + 1) * DH)
                seg = y[:, cs]
                if gi < 2:
                    seg = seg * lax.rsqrt(jnp.sum(seg * seg, axis=-1, keepdims=True) + L2_EPS)
                if gi == 0:
                    seg = seg * (DH ** -0.5)
                dst[c * CHUNK:(c + 1) * CHUNK, cs] = seg

    tail = tail_ref[...]
    g_scr[...] = -jnp.exp(alog_ref[...]) * _softplus(tail + dtb_ref[...])
    b_scr[...] = jax.nn.sigmoid(tail)

    ri = lax.broadcasted_iota(I32, (CHUNK, CHUNK), 0)
    ci_ = lax.broadcasted_iota(I32, (CHUNK, CHUNK), 1)
    causal = ri >= ci_
    strict = ri > ci_
    ltri = jnp.where(causal, 1.0, 0.0).astype(F32)
    same_block = lambda size: (lax.shift_right_logical(ri, size.bit_length() - 1)
                               == lax.shift_right_logical(ci_, size.bit_length() - 1))
    base_mask = same_block(INV_BASE)
    off_masks = []
    size = INV_BASE
    while size < CHUNK:
        off_masks.append(same_block(2 * size) & jnp.logical_not(same_block(size)))
        size *= 2

    def chunk_group(gi, carry):
        rows_c, gcum_c, gcum_t_c, bt_c = [], [], [], []
        for c in range(group):
            rows_c.append(pl.ds(pl.multiple_of((gi * group + c) * CHUNK, CHUNK), CHUNK))
            gcum_c.append(_dot(ltri, g_scr[rows_c[c], :], lax.Precision.HIGHEST))
            gcum_t_c.append(gcum_c[c].T)
            bt_c.append(b_scr[rows_c[c], :])
        units = [(c, h) for c in range(group) for h in range(HEADS)]
        hs = range(len(units))
        rows = [rows_c[c] for c, _ in units]
        cols = [slice(h * DH, (h + 1) * DH) for _, h in units]
        gc = [gcum_c[c][:, h:h + 1] for c, h in units]
        beta = [bt_c[c][:, HEADS + h:HEADS + h + 1] for c, h in units]
        decay = [jnp.where(causal, jnp.exp(jnp.where(causal, gc[u] - gcum_t_c[c][h:h + 1, :], 0.0)), 0.0)
                 for u, (c, h) in enumerate(units)]
        q = [qn_scr[rows[u], cols[u]] for u in hs]
        k = [kn_scr[rows[u], cols[u]] for u in hs]
        kb = [k[h].astype(BF16) for h in hs]
        kq = [_dot_nt(jnp.concatenate([kb[h], q[h].astype(BF16)], axis=0), kb[h]) for h in hs]
        nm = [jnp.where(strict, beta[h] * kq[h][:CHUNK] * decay[h], 0.0) for h in hs]
        pw = [jnp.where(base_mask, nm[h], 0.0) for h in hs]
        e = [-pw[h] for h in hs]
        for _ in range(INV_BASE_LEVELS):
            pwb = [pw[h].astype(BF16) for h in hs]
            pw = [_dot(pwb[h], pwb[h]) for h in hs]
            ep = [_dot(e[h].astype(BF16), pw[h].astype(BF16)) for h in hs]
            e = [e[h] + pw[h] + ep[h] for h in hs]
        for off_mask in off_masks:
            c = [jnp.where(off_mask, nm[h], 0.0) for h in hs]
            x = [c[h] + _dot(e[h].astype(BF16), c[h].astype(BF16)) for h in hs]
            xe = [_dot(x[h].astype(BF16), e[h].astype(BF16)) for h in hs]
            e = [e[h] - (x[h] + xe[h]) for h in hs]
        egc = [jnp.exp(gc[h]) for h in hs]
        rhs = [jnp.concatenate([vn_scr[rows[h], cols[h]] * beta[h], k[h] * (beta[h] * egc[h])], axis=1) for h in hs]
        uw = [rhs[h] + _dot(e[h].astype(BF16), rhs[h].astype(BF16)) for h in hs]
        wq = [jnp.concatenate([uw[h][:, DH:], q[h] * egc[h]], axis=0).astype(BF16) for h in hs]
        g_last = [gcum_c[c][CHUNK - 1:CHUNK, h:h + 1] for c, h in units]
        att_kd = [jnp.concatenate([kq[h][CHUNK:] * decay[h], (k[h] * jnp.exp(g_last[h] - gc[h])).T],
                                  axis=0).astype(BF16) for h in hs]
        for c in range(group):
            us = [u for u in hs if units[u][0] == c]
            s = {u: s_scr[units[u][1]] for u in us}
            ws_qs = {u: _dot(wq[u], s[u].astype(BF16)) for u in us}
            vnb = {u: (uw[u][:, :DH] - ws_qs[u][:CHUNK]).astype(BF16) for u in us}
            av = {u: _dot(att_kd[u], vnb[u]) for u in us}
            for u in us:
                s_scr[units[u][1]] = s[u] * jnp.exp(g_last[u]) + av[u][CHUNK:]
                o = ws_qs[u][CHUNK:] + av[u][:CHUNK]
                o = o * lax.rsqrt(jnp.mean(o * o, axis=-1, keepdims=True) + NORM_EPS) * nw_ref[...]
                o = o * _silu(z_ref[rows[u], cols[u]])
                o_ref[rows[u], cols[u]] = o.astype(o_ref.dtype)
        return carry

    lax.fori_loop(0, nchunk // group, chunk_group, 0)

    @pl.when(tb == pl.num_programs(1) - 1)
    def _():
        sout_ref[0] = s_scr[...]
        ctail_ref[0] = prev_scr[...]


def _gdn_prompt(p, o, bsz, seq, tt, cw, alog_row, dtb_row, nw):
    nchunk = tt // CHUNK
    group = math.gcd(nchunk, GDN_GROUP)
    nt = seq // tt
    n = p.shape[0]
    col = lambda c: pl.BlockSpec((tt, HW), lambda b, t, c=c: (b * nt + t, c))
    full = lambda a: pl.BlockSpec(a.shape, lambda b, t: (0,) * a.ndim)
    return pl.pallas_call(
        functools.partial(_gdn_body, nchunk=nchunk, tt=tt, group=group),
        out_shape=(jax.ShapeDtypeStruct((n, 2 * HW), BF16),
                   jax.ShapeDtypeStruct((bsz, HEADS, DH, DH), F32),
                   jax.ShapeDtypeStruct((bsz, 3, SUBLANES, HW), F32)),
        grid=(bsz, nt),
        in_specs=[col(COL_QD), col(COL_KD), col(COL_VD), col(COL_ZD),
                  pl.BlockSpec((tt, LANES), lambda b, t: (b * nt + t, TAIL_COL // LANES)),
                  pl.BlockSpec(memory_space=pl.ANY),
                  full(cw), full(alog_row), full(dtb_row), full(nw)],
        out_specs=(pl.BlockSpec((tt, HW), lambda b, t: (b * nt + t, 1)),
                   pl.BlockSpec((1, HEADS, DH, DH), lambda b, t: (b, 0, 0, 0)),
                   pl.BlockSpec((1, 3, SUBLANES, HW), lambda b, t: (b, 0, 0, 0))),
        scratch_shapes=[pltpu.VMEM((HEADS, DH, DH), F32),
                        pltpu.VMEM((3, SUBLANES, HW), F32),
                        pltpu.VMEM((SUBLANES + tt, HW), F32),
                        pltpu.VMEM((tt, HW), F32), pltpu.VMEM((tt, HW), F32), pltpu.VMEM((tt, HW), F32),
                        pltpu.VMEM((tt, LANES), F32), pltpu.VMEM((tt, LANES), F32)],
        input_output_aliases={5: 0},
        compiler_params=_cparams(("arbitrary", "arbitrary")),
        name="gdn_prompt",
    )(p, p, p, p, p, o, cw, alog_row, dtb_row, nw)


def _to_col(row, eye_mask):
    return jnp.sum(jnp.where(eye_mask, row, 0.0), axis=1, keepdims=True)


def _dec_body(qr_ref, kr_ref, vr_ref, gr_ref, qd_ref, kd_ref, vd_ref, zd_ref, tail_ref,
              sret_ref, sgdn_ref, sconv_ref, o_in_ref, cos_ref, sin_ref, gam_ref, cw_ref, alog_ref, dtb_ref,
              gnw_ref, gnb_ref, nw_ref,
              o_ref, sret_o, sgdn_o, conv_o,
              qr_s, kr_s, qd_s, kd_s, vd_s, gb_s, o_scr, *, bb):
    del o_in_ref
    step = pl.program_id(0)

    @pl.when(step == 0)
    def _():
        cos2 = cos_ref[...]
        sin2 = sin_ref[...]
        for h in range(HEADS):
            cs = slice(h * DH, (h + 1) * DH)
            qr_s[:, cs] = _rope(qr_ref[:, cs], cos2, sin2)
            kr_s[:, cs] = _rope(kr_ref[:, cs], cos2, sin2) * (DH ** -0.5)
        for gi, (src, dst) in enumerate(((qd_ref, qd_s), (kd_ref, kd_s), (vd_ref, vd_s))):
            gcols = slice(gi * HW, (gi + 1) * HW)
            x = src[...]
            acc = x * cw_ref[CONV_W - 1:CONV_W, gcols]
            for i in range(CONV_W - 1):
                acc = acc + sconv_ref[i, :, gcols] * cw_ref[i:i + 1, gcols]
            for i in range(CONV_W - 2):
                conv_o[i, :, gcols] = sconv_ref[i + 1, :, gcols]
            conv_o[CONV_W - 2, :, gcols] = x
            y = _silu(acc)
            for h in range(HEADS):
                cs = slice(h * DH, (h + 1) * DH)
                seg = y[:, cs]
                if gi < 2:
                    seg = seg * lax.rsqrt(jnp.sum(seg * seg, axis=-1, keepdims=True) + L2_EPS)
                if gi == 0:
                    seg = seg * (DH ** -0.5)
                dst[:, cs] = seg
        tail = tail_ref[...]
        lane = lax.broadcasted_iota(I32, tail.shape, 1)
        g = -jnp.exp(alog_ref[...]) * _softplus(tail + dtb_ref[...])
        gb_s[...] = jnp.where(lane < HEADS, g, jax.nn.sigmoid(tail))

    eye_mask = lax.broadcasted_iota(I32, (DH, DH), 0) == lax.broadcasted_iota(I32, (DH, DH), 1)

    rows8 = pl.ds(pl.multiple_of(step * bb, bb), bb)
    sub = lax.broadcasted_iota(I32, (bb, DH), 0)

    def per_seq(j, o8):
        pick = lambda ref, cs: jnp.sum(jnp.where(sub == j, ref[rows8, cs], 0.0), axis=0, keepdims=True)
        gb = pick(gb_s, slice(0, LANES))
        pieces_r, pieces_d = [], []
        for h in range(HEADS):
            cs = slice(h * DH, (h + 1) * DH)
            s = sret_ref[j, h]
            kcol = _to_col(pick(kr_s, cs), eye_mask)
            qcol = _to_col(pick(qr_s, cs), eye_mask)
            sn = s * gam_ref[h] + kcol * pick(vr_ref, cs)
            sret_o[j, h] = sn
            o = jnp.sum(qcol * sn, axis=0, keepdims=True)
            mu = jnp.mean(o, axis=-1, keepdims=True)
            d = o - mu
            var = jnp.mean(d * d, axis=-1, keepdims=True)
            on = d * lax.rsqrt(var + LN_EPS)
            pieces_r.append(_silu(pick(gr_ref, cs)) * (on * gnw_ref[:, cs] + gnb_ref[:, cs]))
            s = sgdn_ref[j, h]
            eg = jnp.exp(gb[:, h:h + 1])
            beta = gb[:, HEADS + h:HEADS + h + 1]
            kcol = _to_col(pick(kd_s, cs), eye_mask)
            qcol = _to_col(pick(qd_s, cs), eye_mask)
            ks = jnp.sum(kcol * s, axis=0, keepdims=True)
            v_new = beta * (pick(vd_s, cs) - eg * ks)
            sn = s * eg + kcol * v_new
            sgdn_o[j, h] = sn
            o = jnp.sum(qcol * sn, axis=0, keepdims=True)
            o = o * lax.rsqrt(jnp.mean(o * o, axis=-1, keepdims=True) + NORM_EPS) * nw_ref[...]
            pieces_d.append(o * _silu(pick(zd_ref, cs)))
        orow = jnp.concatenate(pieces_r + pieces_d, axis=1)
        return jnp.where(lax.broadcasted_iota(I32, o8.shape, 0) == j, orow, o8)

    o_scr[rows8, :] = lax.fori_loop(0, bb, per_seq, jnp.zeros((bb, 2 * HW), F32))

    @pl.when(step == pl.num_programs(0) - 1)
    def _():
        o_ref[...] = o_scr[...].astype(o_ref.dtype)


def _mixer_sample(p, o, n_prompt, dbs, s_ret, s_gdn, layer, s_conv_t, offset, cw, alog_row, dtb_row, gnw, gnb, nw):
    bb = SUBLANES
    n = p.shape[0]
    rb = n_prompt // dbs
    cos2, sin2 = _rope_tables(1, offset)
    log_gamma = jnp.log(1.0 - 2.0 ** (-5.0 - jnp.arange(HEADS, dtype=F32)))
    gam = jnp.broadcast_to(jnp.exp(log_gamma)[:, None, None], (HEADS, 1, LANES)).astype(F32)
    col = lambda c: pl.BlockSpec((dbs, HW), lambda s, c=c: (rb, c))
    full = lambda a: pl.BlockSpec(a.shape, lambda s: (0,) * a.ndim)
    st = pl.BlockSpec((bb, HEADS, DH, DH), lambda s: (s, 0, 0, 0))
    st_in = pl.BlockSpec((None, bb, HEADS, DH, DH), lambda s: (layer, s, 0, 0, 0))
    vec = lambda: pltpu.VMEM((dbs, HW), F32)
    return pl.pallas_call(
        functools.partial(_dec_body, bb=bb),
        out_shape=(jax.ShapeDtypeStruct((n, 2 * HW), BF16),
                   jax.ShapeDtypeStruct(s_ret.shape[1:], F32),
                   jax.ShapeDtypeStruct(s_gdn.shape[1:], F32),
                   jax.ShapeDtypeStruct(s_conv_t.shape, F32)),
        grid=(dbs // bb,),
        in_specs=[col(COL_QR), col(COL_KR), col(COL_VR), col(COL_GR),
                  col(COL_QD), col(COL_KD), col(COL_VD), col(COL_ZD),
                  pl.BlockSpec((dbs, LANES), lambda s: (rb, TAIL_COL // LANES)),
                  st_in, st_in, full(s_conv_t),
                  pl.BlockSpec(memory_space=pl.ANY),
                  full(cos2), full(sin2), full(gam), full(cw), full(alog_row), full(dtb_row),
                  full(gnw), full(gnb), full(nw)],
        out_specs=(pl.BlockSpec((dbs, 2 * HW), lambda s: (rb, 0)), st, st, full(s_conv_t)),
        scratch_shapes=[vec(), vec(), vec(), vec(), vec(),
                        pltpu.VMEM((dbs, LANES), F32), pltpu.VMEM((dbs, 2 * HW), F32)],
        input_output_aliases={12: 0},
        compiler_params=_cparams(("arbitrary",)),
        name="mixer_sample",
    )(p, p, p, p, p, p, p, p, p, s_ret, s_gdn, s_conv_t, o, cos2, sin2, gam, cw, alog_row, dtb_row,
      gnw, gnb, nw)


def _post_mixer_body(o_ref, x_ref, wout_ref, g_ref, b_ref, wr_ref, br_ref,
                     x1_ref, x1t_ref, e4_ref, r4_ref, g4_ref, cnt_ref, carry_scr, *, alpha, tm):
    i = pl.program_id(0)

    @pl.when(i == 0)
    def _():
        carry_scr[...] = jnp.zeros_like(carry_scr)

    y = _dot(o_ref[...], wout_ref[...])
    x1 = _layer_norm(alpha * x_ref[...] + y, g_ref[...], b_ref[...])
    x1_ref[...] = x1
    _rows_to_tiles(x1t_ref, x1)
    logits = _dot(x1.astype(BF16), wr_ref[...]) + br_ref[...]
    lane = lax.broadcasted_iota(I32, (tm, LANES), 1)
    lane_f = lane.astype(F32)
    work = logits
    ohs, vals, idxs = [], [], []
    for _ in range(TOP_K):
        m = jnp.max(work, axis=1, keepdims=True)
        idx = jnp.min(jnp.where(work == m, lane_f, float(LANES)), axis=1, keepdims=True)
        oh = lane_f == idx
        ohs.append(oh)
        vals.append(m)
        idxs.append(idx)
        work = jnp.where(oh, NEG, work)
    es = [jnp.exp(v - vals[0]) for v in vals]
    den = es[0] + es[1] + es[2] + es[3]
    sel = jnp.zeros((tm, LANES), F32)
    for oh in ohs:
        sel = sel + jnp.where(oh, 1.0, 0.0)
    r_i = lax.broadcasted_iota(I32, (tm, tm), 0)
    c_i = lax.broadcasted_iota(I32, (tm, tm), 1)
    ltri = jnp.where(r_i >= c_i, 1.0, 0.0).astype(BF16)
    incl = _dot(ltri, sel.astype(BF16))
    excl = incl - sel + carry_scr[0:1, :]
    carry_scr[...] = carry_scr[...] + incl[tm - 1:tm, :]
    e4 = jnp.zeros((tm, LANES), F32)
    r4 = jnp.zeros((tm, LANES), F32)
    g4 = jnp.zeros((tm, LANES), F32)
    for k in range(TOP_K):
        rk = jnp.sum(jnp.where(ohs[k], excl, 0.0), axis=1, keepdims=True)
        e4 = jnp.where(lane == k, idxs[k], e4)
        r4 = jnp.where(lane == k, rk, r4)
        g4 = jnp.where(lane == k, es[k] / den, g4)
    e4_ref[...] = e4.astype(I32)
    r4_ref[...] = r4.astype(I32)
    g4_ref[...] = g4
    cnt_ref[...] = carry_scr[...]


def _post_mixer(o, x, wout_bf, ln_g, ln_b, wr_bf, br_row, alpha, tm):
    n, d = x.shape
    row = lambda w: pl.BlockSpec((tm, w), lambda i: (i, 0))
    full = lambda a: pl.BlockSpec(a.shape, lambda i: (0,) * a.ndim)
    return pl.pallas_call(
        functools.partial(_post_mixer_body, alpha=alpha, tm=tm),
        out_shape=(jax.ShapeDtypeStruct((n, d), F32),
                   jax.ShapeDtypeStruct((n * SUBLANES, LANES), F32),
                   jax.ShapeDtypeStruct((n, LANES), I32),
                   jax.ShapeDtypeStruct((n, LANES), I32),
                   jax.ShapeDtypeStruct((n, LANES), F32),
                   jax.ShapeDtypeStruct((SUBLANES, LANES), F32)),
        grid=(n // tm,),
        in_specs=[row(2 * HW), row(d), full(wout_bf), full(ln_g), full(ln_b), full(wr_bf), full(br_row)],
        out_specs=(row(d), pl.BlockSpec((tm * SUBLANES, LANES), lambda i: (i, 0)),
                   row(LANES), row(LANES), row(LANES),
                   pl.BlockSpec((SUBLANES, LANES), lambda i: (0, 0))),
        scratch_shapes=[pltpu.VMEM((SUBLANES, LANES), F32)],
        compiler_params=_cparams(("arbitrary",)),
        name="post_mixer",
    )(o, x, wout_bf, ln_g, ln_b, wr_bf, br_row)


def _plan_body(e4_ref, r4_ref, cnt_ref, d4_ref, blk_ref, seg_ref, pstart_scr, *, tm, n_experts, rows):
    nbp = blk_ref.shape[0]

    @pl.when(pl.program_id(0) == 0)
    def _():
        cnt = cnt_ref[...]
        padded = jnp.floor((cnt + (MOE_BM - 1.0)) * (1.0 / MOE_BM)) * MOE_BM
        m_i = lax.broadcasted_iota(I32, (LANES, LANES), 0)
        j_i = lax.broadcasted_iota(I32, (LANES, LANES), 1)
        upper = jnp.where(m_i <= j_i, 1.0, 0.0).astype(F32)
        pad_end = _dot(padded, upper, lax.Precision.HIGHEST)
        pstart_scr[...] = pad_end - padded
        bstart = lax.broadcasted_iota(I32, (nbp, LANES), 0).astype(F32) * MOE_BM
        blane = lax.broadcasted_iota(I32, (nbp, LANES), 1)
        hit = jnp.where((pad_end[0:1, :] <= bstart) & (blane < n_experts), 1.0, 0.0)
        be = jnp.minimum(jnp.sum(hit, axis=1, keepdims=True), n_experts - 1.0)
        blk_ref[...] = jnp.broadcast_to(be, (nbp, LANES)).astype(I32)
        srow = lax.broadcasted_iota(I32, (SUBLANES, LANES), 0)
        slane = lax.broadcasted_iota(I32, (SUBLANES, LANES), 1)
        total = jnp.sum(jnp.where(slane == n_experts - 1, pad_end, 0.0), axis=1, keepdims=True)
        seg_end = jnp.where(slane == n_experts - 1, float(rows), pad_end)
        seg = jnp.where(srow == 0, pad_end - padded + cnt,
                        jnp.where(srow == 1, seg_end,
                                  jnp.where(srow == 2, total * (1.0 / MOE_BM), pad_end * (1.0 / MOE_BM))))
        seg_ref[...] = seg.astype(I32)

    pad_start = pstart_scr[0:1, :]
    lane = lax.broadcasted_iota(I32, (tm, LANES), 1)
    e4 = e4_ref[...]
    r4 = r4_ref[...].astype(F32)
    d4 = jnp.zeros((tm, LANES), F32)
    for k in range(TOP_K):
        ek = e4[:, k:k + 1]
        ps = jnp.sum(jnp.where(lane == ek, pad_start, 0.0), axis=1, keepdims=True)
        d4 = jnp.where(lane == k, ps + r4[:, k:k + 1], d4)
    d4_ref[...] = d4.astype(I32)


def _plan(e4, r4, cnt, tm, n_experts, rows):
    n = e4.shape[0]
    nbp = pl.cdiv(rows // MOE_BM, SUBLANES) * SUBLANES
    row = pl.BlockSpec((tm, LANES), lambda i: (i, 0))
    return pl.pallas_call(
        functools.partial(_plan_body, tm=tm, n_experts=n_experts, rows=rows),
        out_shape=(jax.ShapeDtypeStruct((n, LANES), I32),
                   jax.ShapeDtypeStruct((nbp, LANES), I32),
                   jax.ShapeDtypeStruct((SUBLANES, LANES), I32)),
        grid=(n // tm,),
        in_specs=[row, row, pl.BlockSpec((SUBLANES, LANES), lambda i: (0, 0))],
        out_specs=(row, pl.BlockSpec((nbp, LANES), lambda i: (0, 0)),
                   pl.BlockSpec((SUBLANES, LANES), lambda i: (0, 0))),
        scratch_shapes=[pltpu.VMEM((SUBLANES, LANES), F32)],
        compiler_params=_cparams(("arbitrary",)),
        name="moe_plan",
    )(e4, r4, cnt)


def _tile_at(ref, r, lead=()):
    return ref.at[lead + (pl.ds(pl.multiple_of(r * SUBLANES, SUBLANES), SUBLANES), slice(None))]


def _issue(copy_of, count):
    def start(i, carry):
        for u in range(DMA_UNROLL):
            copy_of(i * DMA_UNROLL + u).start(priority=u % 2)
        return carry

    lax.fori_loop(0, count // DMA_UNROLL, start, 0)


def _wait(copy_of, count):
    def wait(pidx, carry):
        copy_of(pidx).wait()
        return carry

    lax.fori_loop(0, count, wait, 0, unroll=DMA_UNROLL)


def _dispatch_body(zs_ref, ze_ref, dest_ref, x_ref, xs_ref, zero_scr, sem, zsem, *, n_experts):
    def token_copy(pidx):
        t = lax.shift_right_logical(pidx, 2)
        return pltpu.make_async_copy(_tile_at(x_ref, t), _tile_at(xs_ref, dest_ref[0, 0, pidx]), sem)

    _issue(token_copy, TOK_TILE * TOP_K)
    _wait(token_copy, TOK_TILE * TOP_K)

    @pl.when(pl.program_id(0) == pl.num_programs(0) - 1)
    def _():
        zero_scr[...] = jnp.zeros_like(zero_scr)

        def zero_rows(first, nrows, wait):
            cp = pltpu.make_async_copy(zero_scr.at[pl.ds(0, nrows * SUBLANES), :],
                                       xs_ref.at[pl.ds(pl.multiple_of(first * SUBLANES, SUBLANES),
                                                       nrows * SUBLANES), :], zsem)
            cp.wait() if wait else cp.start()

        def per_expert(wait):
            def body(e, carry):
                first = zs_ref[e]
                count = ze_ref[e] - first
                nbig = lax.shift_right_logical(count, ZERO_ROWS.bit_length() - 1)

                def big(j, c):
                    zero_rows(first + j * ZERO_ROWS, ZERO_ROWS, wait)
                    return c

                lax.fori_loop(0, nbig, big, 0)
                cur = first + nbig * ZERO_ROWS
                piece = ZERO_ROWS // 2
                while piece >= 1:
                    has = lax.bitwise_and(count, piece)

                    @pl.when(has != 0)
                    def _(cur=cur, piece=piece):
                        zero_rows(cur, piece, wait)

                    cur = cur + has
                    piece //= 2
                return carry
            lax.fori_loop(0, n_experts, body, 0)

        per_expert(False)
        per_expert(True)


def _dispatch(x1t, dest3, zstart, zend, rows, n_experts):
    n = x1t.shape[0] // SUBLANES
    npairs = TOK_TILE * TOP_K
    grid_spec = pltpu.PrefetchScalarGridSpec(
        num_scalar_prefetch=2,
        grid=(n // TOK_TILE,),
        in_specs=[pl.BlockSpec((1, 1, npairs), lambda i, zs, ze: (i, 0, 0), memory_space=pltpu.SMEM),
                  pl.BlockSpec((TOK_TILE * SUBLANES, LANES), lambda i, zs, ze: (i, 0))],
        out_specs=pl.BlockSpec(memory_space=pl.ANY),
        scratch_shapes=[pltpu.VMEM((ZERO_ROWS * SUBLANES, LANES), F32), pltpu.SemaphoreType.DMA(()),
                        pltpu.SemaphoreType.DMA(())],
    )
    return pl.pallas_call(
        functools.partial(_dispatch_body, n_experts=n_experts),
        out_shape=jax.ShapeDtypeStruct((rows * SUBLANES, LANES), F32),
        grid_spec=grid_spec,
        compiler_params=_cparams(("arbitrary",)),
        name="moe_dispatch",
    )(zstart, zend, dest3, x1t)


def _ffn_body(blk_ref, nused_ref, segend_ref, xs_ref, bgu_ref, bd_ref, wgu_hbm, wd_hbm, ys_ref,
              wgu_f32, wd_f32, wgu_bf, wd_bf, sems, slot_ref, *, de, layer):
    i = pl.program_id(0)
    nused = nused_ref[0]
    e = blk_ref[i]
    prev = blk_ref[jnp.maximum(i - 1, 0)]

    def weight_copies(expert, slot):
        return (pltpu.make_async_copy(wgu_hbm.at[layer, expert], wgu_f32.at[slot], sems.at[0, slot]),
                pltpu.make_async_copy(wd_hbm.at[layer, expert], wd_f32.at[slot], sems.at[1, slot]))

    @pl.when((i == 0) & (nused > 0))
    def _():
        slot_ref[0] = 0
        for c in weight_copies(e, 0):
            c.start()

    @pl.when((i < nused) & ((i == 0) | (e != prev)))
    def _():
        slot = slot_ref[0]
        for c in weight_copies(e, slot):
            c.wait()
        nxt = segend_ref[e]

        @pl.when(nxt < nused)
        def _():
            for c in weight_copies(blk_ref[nxt], 1 - slot):
                c.start()

        rb = 128
        for r in range(0, wgu_bf.shape[0], rb):
            wgu_bf[r:r + rb, :] = wgu_f32[slot, r:r + rb, :].astype(BF16)
        for r in range(0, wd_bf.shape[0], rb):
            wd_bf[r:r + rb, :] = wd_f32[slot, r:r + rb, :].astype(BF16)
        slot_ref[0] = 1 - slot

    @pl.when(i < nused)
    def _():
        xb = _tiles_to_rows(xs_ref, MOE_BM).astype(BF16)
        gu = _dot(xb, wgu_bf[...]) + bgu_ref[...]
        gate = jnp.minimum(gu[:, :de], SWIGLU_LIMIT)
        up = jnp.clip(gu[:, de:], -SWIGLU_LIMIT, SWIGLU_LIMIT)
        act = (up + 1.0) * gate * jax.nn.sigmoid(SWIGLU_ALPHA * gate)
        _rows_to_tiles(ys_ref, _dot(act.astype(BF16), wd_bf[...]) + bd_ref[...])

    @pl.when(i >= nused)
    def _():
        ys_ref[...] = jnp.zeros_like(ys_ref)


def _ffn(xs, blk_e, nused, segend_blk, w_gu, b_gu, w_down, b_down, layer):
    rows = xs.shape[0] // SUBLANES
    depth, n_experts, d, de2 = w_gu.shape
    de = de2 // 2
    nb = rows // MOE_BM
    tile_blk = pl.BlockSpec((MOE_BM * SUBLANES, LANES), lambda i, blk, nu, se: (i, 0))
    grid_spec = pltpu.PrefetchScalarGridSpec(
        num_scalar_prefetch=3,
        grid=(nb,),
        in_specs=[tile_blk,
                  pl.BlockSpec((None, None, 1, de2), lambda i, blk, nu, se: (layer, blk[i], 0, 0)),
                  pl.BlockSpec((None, None, 1, d), lambda i, blk, nu, se: (layer, blk[i], 0, 0)),
                  pl.BlockSpec(memory_space=pl.ANY),
                  pl.BlockSpec(memory_space=pl.ANY)],
        out_specs=tile_blk,
        scratch_shapes=[pltpu.VMEM((2, d, de2), F32), pltpu.VMEM((2, de, d), F32),
                        pltpu.VMEM((d, de2), BF16), pltpu.VMEM((de, d), BF16),
                        pltpu.SemaphoreType.DMA((2, 2)), pltpu.SMEM((1,), I32)],
    )
    return pl.pallas_call(
        functools.partial(_ffn_body, de=de, layer=layer),
        out_shape=jax.ShapeDtypeStruct((rows * SUBLANES, LANES), F32),
        grid_spec=grid_spec,
        compiler_params=_cparams(("arbitrary",)),
        name="moe_ffn",
    )(blk_e, nused, segend_blk, xs, b_gu.reshape(depth, n_experts, 1, de2),
      b_down.reshape(depth, n_experts, 1, d), w_gu, w_down)


def _combine_body(dest_ref, dest_next_ref, g4_ref, x1_ref, g_ref, b_ref, ys_ref, *rest, alpha, split_tiles):
    *out_refs, buf, sems = rest
    i = pl.program_id(0)
    slot = lax.rem(i, 2)
    npairs = TOK_TILE * TOP_K

    def gather(dref, s):
        def row_copy(pidx):
            t = lax.shift_right_logical(pidx, 2)
            k = lax.bitwise_and(pidx, TOP_K - 1)
            return pltpu.make_async_copy(_tile_at(ys_ref, dref[0, 0, pidx]), _tile_at(buf, t, (s, k)), sems.at[s])
        return row_copy

    @pl.when(i == 0)
    def _():
        _issue(gather(dest_ref, 0), npairs)

    @pl.when(i + 1 < pl.num_programs(0))
    def _():
        _issue(gather(dest_next_ref, 1 - slot), npairs)

    _wait(gather(dest_ref, slot), npairs)
    g4 = g4_ref[...]
    f = g4[:, 0:1] * _tiles_to_rows(buf, TOK_TILE, (slot, 0))
    for k in range(1, TOP_K):
        f = f + g4[:, k:k + 1] * _tiles_to_rows(buf, TOK_TILE, (slot, k))
    res = _layer_norm(alpha * x1_ref[...] + f, g_ref[...], b_ref[...])
    if split_tiles is None:
        out_refs[0][...] = res
    else:

        @pl.when(i < split_tiles)
        def _():
            out_refs[0][...] = res

        @pl.when(i >= split_tiles)
        def _():
            out_refs[1][...] = res


def _combine(ys, dest3, g4, x1, ln_g, ln_b, alpha, n_prompt=None):
    n, d = x1.shape
    full = lambda a: pl.BlockSpec(a.shape, lambda i: (0,) * a.ndim)
    tile = lambda f: pl.BlockSpec((TOK_TILE, d), f)
    if n_prompt is None:
        split_tiles = None
        out_shape = jax.ShapeDtypeStruct((n, d), F32)
        out_specs = tile(lambda i: (i, 0))
    else:
        assert n - n_prompt == TOK_TILE and n_prompt % TOK_TILE == 0
        split_tiles = n_prompt // TOK_TILE
        out_shape = (jax.ShapeDtypeStruct((n_prompt, d), F32), jax.ShapeDtypeStruct((TOK_TILE, d), F32))
        out_specs = (tile(lambda i: (jnp.minimum(i, split_tiles - 1), 0)), tile(lambda i: (0, 0)))
    nsteps = n // TOK_TILE
    dest_spec = lambda f: pl.BlockSpec((1, 1, TOK_TILE * TOP_K), f, memory_space=pltpu.SMEM)
    return pl.pallas_call(
        functools.partial(_combine_body, alpha=alpha, split_tiles=split_tiles),
        out_shape=out_shape,
        grid=(nsteps,),
        in_specs=[dest_spec(lambda i: (i, 0, 0)),
                  dest_spec(lambda i: (jnp.minimum(i + 1, nsteps - 1), 0, 0)),
                  pl.BlockSpec((TOK_TILE, LANES), lambda i: (i, 0)),
                  tile(lambda i: (i, 0)),
                  full(ln_g), full(ln_b),
                  pl.BlockSpec(memory_space=pl.ANY)],
        out_specs=out_specs,
        scratch_shapes=[pltpu.VMEM((2, TOP_K, TOK_TILE * SUBLANES, LANES), F32), pltpu.SemaphoreType.DMA((2,))],
        compiler_params=_cparams(("arbitrary",)),
        name="moe_combine",
    )(dest3, dest3, g4, x1, ln_g, ln_b, ys)


def kernel(x_prompt, x_sample, state_ret, state_gdn, state_conv, w_in, conv_w, a_log, dt_bias, ret_gn_w, ret_gn_b, gdn_norm_w, w_out, ln1_g, ln1_b, w_router, b_router, w_gu, b_gu, w_down, b_down, ln2_g, ln2_b):
    bsz, seq, d = x_prompt.shape
    dbs, dseq, _ = x_sample.shape
    depth = w_in.shape[0]
    n_experts = w_router.shape[-1]
    assert dseq == 1 and dbs == LANES and seq % CHUNK == 0 and d == 2 * HW == SUBLANES * LANES
    assert 2 ** (INV_BASE_LEVELS + 1) == INV_BASE and CHUNK % INV_BASE == 0
    n_prompt = bsz * seq
    n = n_prompt + dbs
    alpha = (2.0 * depth) ** 0.25
    tm = _pick_tile(n, (384, 256, 128))
    tt = _pick_tile(seq, (512, 256, 128))
    rows = (pl.cdiv(n * TOP_K, MOE_BM) + n_experts) * MOE_BM

    x = jnp.concatenate([x_prompt.reshape(n_prompt, d), x_sample.reshape(dbs, d)], axis=0)
    row2 = lambda a: a.reshape(1, -1).astype(F32)
    pad_lanes = lambda a, fill: jnp.concatenate(
        [a.astype(F32), jnp.full((LANES - a.shape[0],), fill, F32)]).reshape(1, LANES)

    ret_p, gdn_p, conv_p, ret_s, gdn_s, conv_s = [], [], [], [], [], []
    for l in range(depth):
        w_in_bf = jnp.pad(w_in[l], ((0, 0), (0, IN_PAD - w_in.shape[-1]))).astype(BF16)
        p = _inproj(x, w_in_bf, tm)
        gnw, gnb, nw = row2(ret_gn_w[l]), row2(ret_gn_b[l]), row2(gdn_norm_w[l])
        alog_row, dtb_row = pad_lanes(a_log[l], 0.0), pad_lanes(dt_bias[l], 0.0)
        cw = conv_w[l].astype(F32)

        o, sr = _retention_prompt(p, n, bsz, seq, tt, gnw, gnb)
        o, sg, ctail = _gdn_prompt(p, o, bsz, seq, tt, cw, alog_row, dtb_row, nw)
        sconv_t = jnp.transpose(state_conv[l], (1, 0, 2))
        o, sr_s, sg_s, sc_s = _mixer_sample(p, o, n_prompt, dbs, state_ret, state_gdn, l, sconv_t,
                                            float(PAST_LEN), cw, alog_row, dtb_row, gnw, gnb, nw)
        ret_p.append(sr)
        gdn_p.append(sg)
        conv_p.append(jnp.transpose(ctail[:, :, SUBLANES - (CONV_W - 1):, :], (0, 2, 1, 3))
                      .reshape(bsz, CONV_W - 1, 3 * HW))
        ret_s.append(sr_s)
        gdn_s.append(sg_s)
        conv_s.append(jnp.transpose(sc_s, (1, 0, 2)))

        wr_bf = jnp.pad(w_router[l], ((0, 0), (0, LANES - n_experts))).astype(BF16)
        br_row = pad_lanes(b_router[l], NEG)
        x1, x1t, e4, r4, g4, cnt = _post_mixer(o, x, w_out[l].astype(BF16), row2(ln1_g[l]), row2(ln1_b[l]),
                                               wr_bf, br_row, alpha, tm)
        d4, blk, seg = _plan(e4, r4, cnt, _largest_tile(n, PLAN_TILE_CAP), n_experts, rows)
        dest3 = d4[:, :TOP_K].reshape(n // TOK_TILE, 1, TOK_TILE * TOP_K)
        xs = _dispatch(x1t, dest3, seg[0, :n_experts], seg[1, :n_experts], rows, n_experts)
        ys = _ffn(xs, blk[:rows // MOE_BM, 0], seg[2, :1], seg[3, :n_experts], w_gu, b_gu, w_down, b_down, l)
        if l + 1 < depth:
            x = _combine(ys, dest3, g4, x1, row2(ln2_g[l]), row2(ln2_b[l]), alpha)
        else:
            y_prompt, y_sample = _combine(ys, dest3, g4, x1, row2(ln2_g[l]), row2(ln2_b[l]), alpha, n_prompt)

    return (y_prompt.reshape(bsz, seq, d), y_sample.reshape(dbs, dseq, d),
            jnp.stack(ret_p), jnp.stack(gdn_p), jnp.stack(conv_p),
            jnp.stack(ret_s), jnp.stack(gdn_s), jnp.stack(conv_s))
```

```python
import functools
import math

import jax
import jax.numpy as jnp
from jax import lax
from jax.experimental import pallas as pl
from jax.experimental.pallas import tpu as pltpu

F32 = jnp.float32
BF16 = jnp.bfloat16
I32 = jnp.int32

HEADS = 4
DH = 128
HW = HEADS * DH
CONV_W = 4
ROPE_BASE = 10000.0
PAST_LEN = 16384
TOP_K = 4
SWIGLU_LIMIT = 7.0
SWIGLU_ALPHA = 1.702
LN_EPS = 1e-5
NORM_EPS = 1e-6
L2_EPS = 1e-6

LANES = 128
SUBLANES = 8
VMEM_LIMIT_BYTES = 56 * 1024 * 1024

CHUNK = 128
GDN_GROUP = 4
INV_BASE = 16
INV_BASE_LEVELS = 3
MOE_BM = 256
TOK_TILE = 128
ZERO_ROWS = 128
PLAN_TILE_CAP = 8192
DMA_UNROLL = 8
NEG = -3.0e38

COL_QR, COL_KR, COL_VR, COL_GR, COL_QD, COL_KD, COL_VD, COL_ZD = range(8)
TAIL_COL = 8 * HW
IN_PAD = TAIL_COL + LANES

_NT = (((1,), (1,)), ((), ()))


def _cparams(sem, vmem=VMEM_LIMIT_BYTES):
    return pltpu.CompilerParams(dimension_semantics=sem, vmem_limit_bytes=vmem)


def _dot(a, b, precision=None):
    return jnp.dot(a, b, preferred_element_type=F32, precision=precision)


def _dot_nt(a, b):
    return lax.dot_general(a, b, _NT, preferred_element_type=F32)


def _silu(x):
    return x * jax.nn.sigmoid(x)


def _layer_norm(h, g, b):
    mu = jnp.mean(h, axis=-1, keepdims=True)
    d = h - mu
    var = jnp.mean(d * d, axis=-1, keepdims=True)
    return d * lax.rsqrt(var + LN_EPS) * g + b


def _rope(x, cos2, sin2):
    return x * cos2 + pltpu.roll(x, DH // 2, 1) * sin2


def _tiles_to_rows(ref, nrows, lead=()):
    return jnp.concatenate([ref[lead + (pl.ds(s, nrows, stride=SUBLANES), slice(None))] for s in range(SUBLANES)],
                           axis=1)


def _rows_to_tiles(ref, val):
    nrows = val.shape[0]
    for s in range(SUBLANES):
        ref[pl.ds(s, nrows, stride=SUBLANES), :] = val[:, s * LANES:(s + 1) * LANES]


def _pick_tile(n, candidates):
    for c in candidates:
        if n % c == 0:
            return c
    raise ValueError(f"no tile in {candidates} divides {n}")


def _largest_tile(n, cap):
    for parts in range(1, n + 1):
        if n % parts == 0 and (n // parts) % SUBLANES == 0 and n // parts <= cap:
            return n // parts
    raise ValueError(f"no tile of at most {cap} rows divides {n}")


def _inproj_body(x_ref, w_ref, o_ref):
    o_ref[...] = _dot(x_ref[...].astype(BF16), w_ref[...])


def _inproj(x, w_bf, tm):
    n, d = x.shape
    return pl.pallas_call(
        _inproj_body,
        out_shape=jax.ShapeDtypeStruct((n, IN_PAD), F32),
        grid=(n // tm,),
        in_specs=[pl.BlockSpec((tm, d), lambda i: (i, 0)),
                  pl.BlockSpec((d, IN_PAD), lambda i: (0, 0))],
        out_specs=pl.BlockSpec((tm, IN_PAD), lambda i: (i, 0)),
        compiler_params=_cparams(("arbitrary",)),
        name="inproj",
    )(x, w_bf)


def _ret_body(q_ref, k_ref, v_ref, g_ref, o_in_ref, cos_ref, sin_ref, intra_ref, qdec_ref, kdec_ref, cdec_ref,
              gnw_ref, gnb_ref, o_ref, sout_ref, s_scr, *, nchunk):
    del o_in_ref
    tb = pl.program_id(1)

    @pl.when(tb == 0)
    def _():
        s_scr[...] = jnp.zeros_like(s_scr)

    def chunk(ci, carry):
        rows = pl.ds(pl.multiple_of(ci * CHUNK, CHUNK), CHUNK)
        cos2 = cos_ref[rows, :]
        sin2 = sin_ref[rows, :]
        hs = range(HEADS)
        cols = [slice(h * DH, (h + 1) * DH) for h in hs]
        q = [_rope(q_ref[rows, cols[h]], cos2, sin2) for h in hs]
        k = [_rope(k_ref[rows, cols[h]], cos2, sin2) * (DH ** -0.5) for h in hs]
        vb = [v_ref[rows, cols[h]].astype(BF16) for h in hs]
        s = [s_scr[h] for h in hs]
        att = [_dot_nt(q[h].astype(BF16), k[h].astype(BF16)) * intra_ref[h] for h in hs]
        qs = [_dot((q[h] * qdec_ref[h]).astype(BF16), s[h].astype(BF16)) for h in hs]
        av = [_dot(jnp.concatenate([att[h], (k[h] * kdec_ref[h]).T], axis=0).astype(BF16), vb[h]) for h in hs]
        for h in hs:
            s_scr[h] = s[h] * cdec_ref[h] + av[h][CHUNK:]
            o = av[h][:CHUNK] + qs[h]
            mu = jnp.mean(o, axis=-1, keepdims=True)
            d = o - mu
            var = jnp.mean(d * d, axis=-1, keepdims=True)
            on = d * lax.rsqrt(var + LN_EPS)
            res = _silu(g_ref[rows, cols[h]]) * (on * gnw_ref[:, cols[h]] + gnb_ref[:, cols[h]])
            o_ref[rows, cols[h]] = res.astype(o_ref.dtype)
        return carry

    lax.fori_loop(0, nchunk, chunk, 0)

    @pl.when(tb == pl.num_programs(1) - 1)
    def _():
        sout_ref[0] = s_scr[...]


def _ret_tables(c):
    log_gamma = jnp.log(1.0 - 2.0 ** (-5.0 - jnp.arange(HEADS, dtype=F32)))
    idx = jnp.arange(c, dtype=F32)
    diff = idx[:, None] - idx[None, :]
    causal = diff >= 0
    intra = jnp.where(causal, jnp.exp(log_gamma[:, None, None] * jnp.where(causal, diff, 0.0)), 0.0)
    q_dec = jnp.exp(log_gamma[:, None] * (idx + 1.0))[..., None]
    k_dec = jnp.exp(log_gamma[:, None] * (c - 1.0 - idx))[..., None]
    c_dec = jnp.exp(log_gamma * c)[:, None, None]
    bc = lambda a, r: jnp.broadcast_to(a, (HEADS, r, LANES)).astype(F32)
    return intra, bc(q_dec, c), bc(k_dec, c), bc(c_dec, 1)


def _rope_tables(t, offset):
    half = DH // 2
    inv_freq = ROPE_BASE ** (-jnp.arange(half, dtype=F32) / half)
    pos = jnp.arange(t, dtype=F32) + offset
    ang = pos[:, None] * inv_freq[None, :]
    cos, sin = jnp.cos(ang), jnp.sin(ang)
    return jnp.concatenate([cos, cos], -1), jnp.concatenate([-sin, sin], -1)


def _retention_prompt(p, n, bsz, seq, tt, gnw, gnb):
    nchunk = tt // CHUNK
    nt = seq // tt
    cos2, sin2 = _rope_tables(seq, 0)
    intra, qdec, kdec, cdec = _ret_tables(CHUNK)
    col = lambda c: pl.BlockSpec((tt, HW), lambda b, t, c=c: (b * nt + t, c))
    full = lambda a: pl.BlockSpec(a.shape, lambda b, t: (0,) * a.ndim)
    return pl.pallas_call(
        functools.partial(_ret_body, nchunk=nchunk),
        out_shape=(jax.ShapeDtypeStruct((n, 2 * HW), BF16),
                   jax.ShapeDtypeStruct((bsz, HEADS, DH, DH), F32)),
        grid=(bsz, nt),
        in_specs=[col(COL_QR), col(COL_KR), col(COL_VR), col(COL_GR),
                  pl.BlockSpec(memory_space=pl.ANY),
                  pl.BlockSpec((tt, DH), lambda b, t: (t, 0)),
                  pl.BlockSpec((tt, DH), lambda b, t: (t, 0)),
                  full(intra), full(qdec), full(kdec), full(cdec), full(gnw), full(gnb)],
        out_specs=(pl.BlockSpec((tt, HW), lambda b, t: (b * nt + t, 0)),
                   pl.BlockSpec((1, HEADS, DH, DH), lambda b, t: (b, 0, 0, 0))),
        scratch_shapes=[pltpu.VMEM((HEADS, DH, DH), F32)],
        input_output_aliases={4: 0},
        compiler_params=_cparams(("arbitrary", "arbitrary")),
        name="retention_prompt",
    )(p, p, p, p, jnp.zeros((n, 2 * HW), BF16), cos2, sin2, intra, qdec, kdec, cdec, gnw, gnb)


def _softplus(x):
    return jnp.maximum(x, 0.0) + jnp.log1p(jnp.exp(-jnp.abs(x)))


def _gdn_body(qd_ref, kd_ref, vd_ref, z_ref, tail_ref, o_in_ref, cw_ref, alog_ref, dtb_ref, nw_ref,
              o_ref, sout_ref, ctail_ref, s_scr, prev_scr, xe_scr, qn_scr, kn_scr, vn_scr, g_scr, b_scr,
              *, nchunk, tt, group):
    del o_in_ref
    tb = pl.program_id(1)

    @pl.when(tb == 0)
    def _():
        s_scr[...] = jnp.zeros_like(s_scr)
        prev_scr[...] = jnp.zeros_like(prev_scr)

    for gi, (src, dst) in enumerate(((qd_ref, qn_scr), (kd_ref, kn_scr), (vd_ref, vn_scr))):
        gcols = slice(gi * HW, (gi + 1) * HW)
        xe_scr[0:SUBLANES, :] = prev_scr[gi]
        xe_scr[SUBLANES:SUBLANES + tt, :] = src[...]
        prev_scr[gi] = src[tt - SUBLANES:tt, :]
        for c in range(nchunk):
            base = SUBLANES + c * CHUNK
            acc = xe_scr[base:base + CHUNK, :] * cw_ref[CONV_W - 1:CONV_W, gcols]
            for j in range(1, CONV_W):
                acc = acc + xe_scr[base - j:base - j + CHUNK, :] * cw_ref[CONV_W - 1 - j:CONV_W - j, gcols]
            y = _silu(acc)
            for h in range(HEADS):
                cs = slice(h * DH, (h + 1) * DH)
                seg = y[:, cs]
                if gi < 2:
                    seg = seg * lax.rsqrt(jnp.sum(seg * seg, axis=-1, keepdims=True) + L2_EPS)
                if gi == 0:
                    seg = seg * (DH ** -0.5)
                dst[c * CHUNK:(c + 1) * CHUNK, cs] = seg

    tail = tail_ref[...]
    g_scr[...] = -jnp.exp(alog_ref[...]) * _softplus(tail + dtb_ref[...])
    b_scr[...] = jax.nn.sigmoid(tail)

    ri = lax.broadcasted_iota(I32, (CHUNK, CHUNK), 0)
    ci_ = lax.broadcasted_iota(I32, (CHUNK, CHUNK), 1)
    causal = ri >= ci_
    strict = ri > ci_
    ltri = jnp.where(causal, 1.0, 0.0).astype(F32)
    same_block = lambda size: (lax.shift_right_logical(ri, size.bit_length() - 1)
                               == lax.shift_right_logical(ci_, size.bit_length() - 1))
    base_mask = same_block(INV_BASE)
    off_masks = []
    size = INV_BASE
    while size < CHUNK:
        off_masks.append(same_block(2 * size) & jnp.logical_not(same_block(size)))
        size *= 2

    def chunk_group(gi, carry):
        rows_c, gcum_c, gcum_t_c, bt_c = [], [], [], []
        for c in range(group):
            rows_c.append(pl.ds(pl.multiple_of((gi * group + c) * CHUNK, CHUNK), CHUNK))
            gcum_c.append(_dot(ltri, g_scr[rows_c[c], :], lax.Precision.HIGHEST))
            gcum_t_c.append(gcum_c[c].T)
            bt_c.append(b_scr[rows_c[c], :])
        units = [(c, h) for c in range(group) for h in range(HEADS)]
        hs = range(len(units))
        rows = [rows_c[c] for c, _ in units]
        cols = [slice(h * DH, (h + 1) * DH) for _, h in units]
        gc = [gcum_c[c][:, h:h + 1] for c, h in units]
        beta = [bt_c[c][:, HEADS + h:HEADS + h + 1] for c, h in units]
        decay = [jnp.where(causal, jnp.exp(jnp.where(causal, gc[u] - gcum_t_c[c][h:h + 1, :], 0.0)), 0.0)
                 for u, (c, h) in enumerate(units)]
        q = [qn_scr[rows[u], cols[u]] for u in hs]
        k = [kn_scr[rows[u], cols[u]] for u in hs]
        kb = [k[h].astype(BF16) for h in hs]
        kq = [_dot_nt(jnp.concatenate([kb[h], q[h].astype(BF16)], axis=0), kb[h]) for h in hs]
        nm = [jnp.where(strict, beta[h] * kq[h][:CHUNK] * decay[h], 0.0) for h in hs]
        pw = [jnp.where(base_mask, nm[h], 0.0) for h in hs]
        e = [-pw[h] for h in hs]
        for _ in range(INV_BASE_LEVELS):
            pwb = [pw[h].astype(BF16) for h in hs]
            pw = [_dot(pwb[h], pwb[h]) for h in hs]
            ep = [_dot(e[h].astype(BF16), pw[h].astype(BF16)) for h in hs]
            e = [e[h] + pw[h] + ep[h] for h in hs]
        for off_mask in off_masks:
            c = [jnp.where(off_mask, nm[h], 0.0) for h in hs]
            x = [c[h] + _dot(e[h].astype(BF16), c[h].astype(BF16)) for h in hs]
            xe = [_dot(x[h].astype(BF16), e[h].astype(BF16)) for h in hs]
            e = [e[h] - (x[h] + xe[h]) for h in hs]
        egc = [jnp.exp(gc[h]) for h in hs]
        rhs = [jnp.concatenate([vn_scr[rows[h], cols[h]] * beta[h], k[h] * (beta[h] * egc[h])], axis=1) for h in hs]
        uw = [rhs[h] + _dot(e[h].astype(BF16), rhs[h].astype(BF16)) for h in hs]
        wq = [jnp.concatenate([uw[h][:, DH:], q[h] * egc[h]], axis=0).astype(BF16) for h in hs]
        g_last = [gcum_c[c][CHUNK - 1:CHUNK, h:h + 1] for c, h in units]
        att_kd = [jnp.concatenate([kq[h][CHUNK:] * decay[h], (k[h] * jnp.exp(g_last[h] - gc[h])).T],
                                  axis=0).astype(BF16) for h in hs]
        for c in range(group):
            us = [u for u in hs if units[u][0] == c]
            s = {u: s_scr[units[u][1]] for u in us}
            ws_qs = {u: _dot(wq[u], s[u].astype(BF16)) for u in us}
            vnb = {u: (uw[u][:, :DH] - ws_qs[u][:CHUNK]).astype(BF16) for u in us}
            av = {u: _dot(att_kd[u], vnb[u]) for u in us}
            for u in us:
                s_scr[units[u][1]] = s[u] * jnp.exp(g_last[u]) + av[u][CHUNK:]
                o = ws_qs[u][CHUNK:] + av[u][:CHUNK]
                o = o * lax.rsqrt(jnp.mean(o * o, axis=-1, keepdims=True) + NORM_EPS) * nw_ref[...]
                o = o * _silu(z_ref[rows[u], cols[u]])
                o_ref[rows[u], cols[u]] = o.astype(o_ref.dtype)
        return carry

    lax.fori_loop(0, nchunk // group, chunk_group, 0)

    @pl.when(tb == pl.num_programs(1) - 1)
    def _():
        sout_ref[0] = s_scr[...]
        ctail_ref[0] = prev_scr[...]


def _gdn_prompt(p, o, bsz, seq, tt, cw, alog_row, dtb_row, nw):
    nchunk = tt // CHUNK
    group = math.gcd(nchunk, GDN_GROUP)
    nt = seq // tt
    n = p.shape[0]
    col = lambda c: pl.BlockSpec((tt, HW), lambda b, t, c=c: (b * nt + t, c))
    full = lambda a: pl.BlockSpec(a.shape, lambda b, t: (0,) * a.ndim)
    return pl.pallas_call(
        functools.partial(_gdn_body, nchunk=nchunk, tt=tt, group=group),
        out_shape=(jax.ShapeDtypeStruct((n, 2 * HW), BF16),
                   jax.ShapeDtypeStruct((bsz, HEADS, DH, DH), F32),
                   jax.ShapeDtypeStruct((bsz, 3, SUBLANES, HW), F32)),
        grid=(bsz, nt),
        in_specs=[col(COL_QD), col(COL_KD), col(COL_VD), col(COL_ZD),
                  pl.BlockSpec((tt, LANES), lambda b, t: (b * nt + t, TAIL_COL // LANES)),
                  pl.BlockSpec(memory_space=pl.ANY),
                  full(cw), full(alog_row), full(dtb_row), full(nw)],
        out_specs=(pl.BlockSpec((tt, HW), lambda b, t: (b * nt + t, 1)),
                   pl.BlockSpec((1, HEADS, DH, DH), lambda b, t: (b, 0, 0, 0)),
                   pl.BlockSpec((1, 3, SUBLANES, HW), lambda b, t: (b, 0, 0, 0))),
        scratch_shapes=[pltpu.VMEM((HEADS, DH, DH), F32),
                        pltpu.VMEM((3, SUBLANES, HW), F32),
                        pltpu.VMEM((SUBLANES + tt, HW), F32),
                        pltpu.VMEM((tt, HW), F32), pltpu.VMEM((tt, HW), F32), pltpu.VMEM((tt, HW), F32),
                        pltpu.VMEM((tt, LANES), F32), pltpu.VMEM((tt, LANES), F32)],
        input_output_aliases={5: 0},
        compiler_params=_cparams(("arbitrary", "arbitrary")),
        name="gdn_prompt",
    )(p, p, p, p, p, o, cw, alog_row, dtb_row, nw)


def _to_col(row, eye_mask):
    return jnp.sum(jnp.where(eye_mask, row, 0.0), axis=1, keepdims=True)


def _dec_body(qr_ref, kr_ref, vr_ref, gr_ref, qd_ref, kd_ref, vd_ref, zd_ref, tail_ref,
              sret_ref, sgdn_ref, sconv_ref, o_in_ref, cos_ref, sin_ref, gam_ref, cw_ref, alog_ref, dtb_ref,
              gnw_ref, gnb_ref, nw_ref,
              o_ref, sret_o, sgdn_o, conv_o,
              qr_s, kr_s, qd_s, kd_s, vd_s, gb_s, o_scr, *, bb):
    del o_in_ref
    step = pl.program_id(0)

    @pl.when(step == 0)
    def _():
        cos2 = cos_ref[...]
        sin2 = sin_ref[...]
        for h in range(HEADS):
            cs = slice(h * DH, (h + 1) * DH)
            qr_s[:, cs] = _rope(qr_ref[:, cs], cos2, sin2)
            kr_s[:, cs] = _rope(kr_ref[:, cs], cos2, sin2) * (DH ** -0.5)
        for gi, (src, dst) in enumerate(((qd_ref, qd_s), (kd_ref, kd_s), (vd_ref, vd_s))):
            gcols = slice(gi * HW, (gi + 1) * HW)
            x = src[...]
            acc = x * cw_ref[CONV_W - 1:CONV_W, gcols]
            for i in range(CONV_W - 1):
                acc = acc + sconv_ref[i, :, gcols] * cw_ref[i:i + 1, gcols]
            for i in range(CONV_W - 2):
                conv_o[i, :, gcols] = sconv_ref[i + 1, :, gcols]
            conv_o[CONV_W - 2, :, gcols] = x
            y = _silu(acc)
            for h in range(HEADS):
                cs = slice(h * DH, (h + 1) * DH)
                seg = y[:, cs]
                if gi < 2:
                    seg = seg * lax.rsqrt(jnp.sum(seg * seg, axis=-1, keepdims=True) + L2_EPS)
                if gi == 0:
                    seg = seg * (DH ** -0.5)
                dst[:, cs] = seg
        tail = tail_ref[...]
        lane = lax.broadcasted_iota(I32, tail.shape, 1)
        g = -jnp.exp(alog_ref[...]) * _softplus(tail + dtb_ref[...])
        gb_s[...] = jnp.where(lane < HEADS, g, jax.nn.sigmoid(tail))

    eye_mask = lax.broadcasted_iota(I32, (DH, DH), 0) == lax.broadcasted_iota(I32, (DH, DH), 1)

    rows8 = pl.ds(pl.multiple_of(step * bb, bb), bb)
    sub = lax.broadcasted_iota(I32, (bb, DH), 0)

    def per_seq(j, o8):
        pick = lambda ref, cs: jnp.sum(jnp.where(sub == j, ref[rows8, cs], 0.0), axis=0, keepdims=True)
        gb = pick(gb_s, slice(0, LANES))
        hs = range(HEADS)
        cols = [slice(h * DH, (h + 1) * DH) for h in hs]
        kcr = [_to_col(pick(kr_s, cols[h]), eye_mask) for h in hs]
        qcr = [_to_col(pick(qr_s, cols[h]), eye_mask) for h in hs]
        kcd = [_to_col(pick(kd_s, cols[h]), eye_mask) for h in hs]
        qcd = [_to_col(pick(qd_s, cols[h]), eye_mask) for h in hs]
        sr = [sret_ref[j, h] for h in hs]
        sd = [sgdn_ref[j, h] for h in hs]
        snr = [sr[h] * gam_ref[h] + kcr[h] * pick(vr_ref, cols[h]) for h in hs]
        eg = [jnp.exp(gb[:, h:h + 1]) for h in hs]
        ks = [jnp.sum(kcd[h] * sd[h], axis=0, keepdims=True) for h in hs]
        v_new = [gb[:, HEADS + h:HEADS + h + 1] * (pick(vd_s, cols[h]) - eg[h] * ks[h]) for h in hs]
        snd = [sd[h] * eg[h] + kcd[h] * v_new[h] for h in hs]
        for h in hs:
            sret_o[j, h] = snr[h]
            sgdn_o[j, h] = snd[h]
        o_r = [jnp.sum(qcr[h] * snr[h], axis=0, keepdims=True) for h in hs]
        o_d = [jnp.sum(qcd[h] * snd[h], axis=0, keepdims=True) for h in hs]
        pieces_r, pieces_d = [], []
        for h in hs:
            mu = jnp.mean(o_r[h], axis=-1, keepdims=True)
            d = o_r[h] - mu
            var = jnp.mean(d * d, axis=-1, keepdims=True)
            on = d * lax.rsqrt(var + LN_EPS)
            pieces_r.append(_silu(pick(gr_ref, cols[h])) * (on * gnw_ref[:, cols[h]] + gnb_ref[:, cols[h]]))
            o = o_d[h] * lax.rsqrt(jnp.mean(o_d[h] * o_d[h], axis=-1, keepdims=True) + NORM_EPS) * nw_ref[...]
            pieces_d.append(o * _silu(pick(zd_ref, cols[h])))
        orow = jnp.concatenate(pieces_r + pieces_d, axis=1)
        return jnp.where(lax.broadcasted_iota(I32, o8.shape, 0) == j, orow, o8)

    o_scr[rows8, :] = lax.fori_loop(0, bb, per_seq, jnp.zeros((bb, 2 * HW), F32))

    @pl.when(step == pl.num_programs(0) - 1)
    def _():
        o_ref[...] = o_scr[...].astype(o_ref.dtype)


def _mixer_sample(p, o, n_prompt, dbs, s_ret, s_gdn, layer, s_conv_t, offset, cw, alog_row, dtb_row, gnw, gnb, nw):
    bb = SUBLANES
    n = p.shape[0]
    rb = n_prompt // dbs
    cos2, sin2 = _rope_tables(1, offset)
    log_gamma = jnp.log(1.0 - 2.0 ** (-5.0 - jnp.arange(HEADS, dtype=F32)))
    gam = jnp.broadcast_to(jnp.exp(log_gamma)[:, None, None], (HEADS, 1, LANES)).astype(F32)
    col = lambda c: pl.BlockSpec((dbs, HW), lambda s, c=c: (rb, c))
    full = lambda a: pl.BlockSpec(a.shape, lambda s: (0,) * a.ndim)
    st = pl.BlockSpec((bb, HEADS, DH, DH), lambda s: (s, 0, 0, 0))
    st_in = pl.BlockSpec((None, bb, HEADS, DH, DH), lambda s: (layer, s, 0, 0, 0))
    vec = lambda: pltpu.VMEM((dbs, HW), F32)
    return pl.pallas_call(
        functools.partial(_dec_body, bb=bb),
        out_shape=(jax.ShapeDtypeStruct((n, 2 * HW), BF16),
                   jax.ShapeDtypeStruct(s_ret.shape[1:], F32),
                   jax.ShapeDtypeStruct(s_gdn.shape[1:], F32),
                   jax.ShapeDtypeStruct(s_conv_t.shape, F32)),
        grid=(dbs // bb,),
        in_specs=[col(COL_QR), col(COL_KR), col(COL_VR), col(COL_GR),
                  col(COL_QD), col(COL_KD), col(COL_VD), col(COL_ZD),
                  pl.BlockSpec((dbs, LANES), lambda s: (rb, TAIL_COL // LANES)),
                  st_in, st_in, full(s_conv_t),
                  pl.BlockSpec(memory_space=pl.ANY),
                  full(cos2), full(sin2), full(gam), full(cw), full(alog_row), full(dtb_row),
                  full(gnw), full(gnb), full(nw)],
        out_specs=(pl.BlockSpec((dbs, 2 * HW), lambda s: (rb, 0)), st, st, full(s_conv_t)),
        scratch_shapes=[vec(), vec(), vec(), vec(), vec(),
                        pltpu.VMEM((dbs, LANES), F32), pltpu.VMEM((dbs, 2 * HW), F32)],
        input_output_aliases={12: 0},
        compiler_params=_cparams(("arbitrary",)),
        name="mixer_sample",
    )(p, p, p, p, p, p, p, p, p, s_ret, s_gdn, s_conv_t, o, cos2, sin2, gam, cw, alog_row, dtb_row,
      gnw, gnb, nw)


def _post_mixer_body(o_ref, x_ref, wout_ref, g_ref, b_ref, wr_ref, br_ref,
                     x1_ref, x1t_ref, e4_ref, r4_ref, g4_ref, cnt_ref, carry_scr, *, alpha, tm):
    i = pl.program_id(0)

    @pl.when(i == 0)
    def _():
        carry_scr[...] = jnp.zeros_like(carry_scr)

    y = _dot(o_ref[...], wout_ref[...])
    x1 = _layer_norm(alpha * x_ref[...] + y, g_ref[...], b_ref[...])
    x1_ref[...] = x1
    _rows_to_tiles(x1t_ref, x1)
    logits = _dot(x1.astype(BF16), wr_ref[...]) + br_ref[...]
    lane = lax.broadcasted_iota(I32, (tm, LANES), 1)
    lane_f = lane.astype(F32)
    work = logits
    ohs, vals, idxs = [], [], []
    for _ in range(TOP_K):
        m = jnp.max(work, axis=1, keepdims=True)
        idx = jnp.min(jnp.where(work == m, lane_f, float(LANES)), axis=1, keepdims=True)
        oh = lane_f == idx
        ohs.append(oh)
        vals.append(m)
        idxs.append(idx)
        work = jnp.where(oh, NEG, work)
    es = [jnp.exp(v - vals[0]) for v in vals]
    den = es[0] + es[1] + es[2] + es[3]
    sel = jnp.zeros((tm, LANES), F32)
    for oh in ohs:
        sel = sel + jnp.where(oh, 1.0, 0.0)
    r_i = lax.broadcasted_iota(I32, (tm, tm), 0)
    c_i = lax.broadcasted_iota(I32, (tm, tm), 1)
    ltri = jnp.where(r_i >= c_i, 1.0, 0.0).astype(BF16)
    incl = _dot(ltri, sel.astype(BF16))
    excl = incl - sel + carry_scr[0:1, :]
    carry_scr[...] = carry_scr[...] + incl[tm - 1:tm, :]
    e4 = jnp.zeros((tm, LANES), F32)
    r4 = jnp.zeros((tm, LANES), F32)
    g4 = jnp.zeros((tm, LANES), F32)
    for k in range(TOP_K):
        rk = jnp.sum(jnp.where(ohs[k], excl, 0.0), axis=1, keepdims=True)
        e4 = jnp.where(lane == k, idxs[k], e4)
        r4 = jnp.where(lane == k, rk, r4)
        g4 = jnp.where(lane == k, es[k] / den, g4)
    e4_ref[...] = e4.astype(I32)
    r4_ref[...] = r4.astype(I32)
    g4_ref[...] = g4
    cnt_ref[...] = carry_scr[...]


def _post_mixer(o, x, wout_bf, ln_g, ln_b, wr_bf, br_row, alpha, tm):
    n, d = x.shape
    row = lambda w: pl.BlockSpec((tm, w), lambda i: (i, 0))
    full = lambda a: pl.BlockSpec(a.shape, lambda i: (0,) * a.ndim)
    return pl.pallas_call(
        functools.partial(_post_mixer_body, alpha=alpha, tm=tm),
        out_shape=(jax.ShapeDtypeStruct((n, d), F32),
                   jax.ShapeDtypeStruct((n * SUBLANES, LANES), F32),
                   jax.ShapeDtypeStruct((n, LANES), I32),
                   jax.ShapeDtypeStruct((n, LANES), I32),
                   jax.ShapeDtypeStruct((n, LANES), F32),
                   jax.ShapeDtypeStruct((SUBLANES, LANES), F32)),
        grid=(n // tm,),
        in_specs=[row(2 * HW), row(d), full(wout_bf), full(ln_g), full(ln_b), full(wr_bf), full(br_row)],
        out_specs=(row(d), pl.BlockSpec((tm * SUBLANES, LANES), lambda i: (i, 0)),
                   row(LANES), row(LANES), row(LANES),
                   pl.BlockSpec((SUBLANES, LANES), lambda i: (0, 0))),
        scratch_shapes=[pltpu.VMEM((SUBLANES, LANES), F32)],
        compiler_params=_cparams(("arbitrary",)),
        name="post_mixer",
    )(o, x, wout_bf, ln_g, ln_b, wr_bf, br_row)


def _plan_body(e4_ref, r4_ref, cnt_ref, d4_ref, blk_ref, seg_ref, pstart_scr, *, tm, n_experts, rows):
    nbp = blk_ref.shape[0]

    @pl.when(pl.program_id(0) == 0)
    def _():
        cnt = cnt_ref[...]
        padded = jnp.floor((cnt + (MOE_BM - 1.0)) * (1.0 / MOE_BM)) * MOE_BM
        m_i = lax.broadcasted_iota(I32, (LANES, LANES), 0)
        j_i = lax.broadcasted_iota(I32, (LANES, LANES), 1)
        upper = jnp.where(m_i <= j_i, 1.0, 0.0).astype(F32)
        pad_end = _dot(padded, upper, lax.Precision.HIGHEST)
        pstart_scr[...] = pad_end - padded
        bstart = lax.broadcasted_iota(I32, (nbp, LANES), 0).astype(F32) * MOE_BM
        blane = lax.broadcasted_iota(I32, (nbp, LANES), 1)
        hit = jnp.where((pad_end[0:1, :] <= bstart) & (blane < n_experts), 1.0, 0.0)
        be = jnp.minimum(jnp.sum(hit, axis=1, keepdims=True), n_experts - 1.0)
        blk_ref[...] = jnp.broadcast_to(be, (nbp, LANES)).astype(I32)
        srow = lax.broadcasted_iota(I32, (SUBLANES, LANES), 0)
        slane = lax.broadcasted_iota(I32, (SUBLANES, LANES), 1)
        total = jnp.sum(jnp.where(slane == n_experts - 1, pad_end, 0.0), axis=1, keepdims=True)
        seg_end = jnp.where(slane == n_experts - 1, float(rows), pad_end)
        seg = jnp.where(srow == 0, pad_end - padded + cnt,
                        jnp.where(srow == 1, seg_end,
                                  jnp.where(srow == 2, total * (1.0 / MOE_BM), pad_end * (1.0 / MOE_BM))))
        seg_ref[...] = seg.astype(I32)

    pad_start = pstart_scr[0:1, :]
    lane = lax.broadcasted_iota(I32, (tm, LANES), 1)
    e4 = e4_ref[...]
    r4 = r4_ref[...].astype(F32)
    d4 = jnp.zeros((tm, LANES), F32)
    for k in range(TOP_K):
        ek = e4[:, k:k + 1]
        ps = jnp.sum(jnp.where(lane == ek, pad_start, 0.0), axis=1, keepdims=True)
        d4 = jnp.where(lane == k, ps + r4[:, k:k + 1], d4)
    d4_ref[...] = d4.astype(I32)


def _plan(e4, r4, cnt, tm, n_experts, rows):
    n = e4.shape[0]
    nbp = pl.cdiv(rows // MOE_BM, SUBLANES) * SUBLANES
    row = pl.BlockSpec((tm, LANES), lambda i: (i, 0))
    return pl.pallas_call(
        functools.partial(_plan_body, tm=tm, n_experts=n_experts, rows=rows),
        out_shape=(jax.ShapeDtypeStruct((n, LANES), I32),
                   jax.ShapeDtypeStruct((nbp, LANES), I32),
                   jax.ShapeDtypeStruct((SUBLANES, LANES), I32)),
        grid=(n // tm,),
        in_specs=[row, row, pl.BlockSpec((SUBLANES, LANES), lambda i: (0, 0))],
        out_specs=(row, pl.BlockSpec((nbp, LANES), lambda i: (0, 0)),
                   pl.BlockSpec((SUBLANES, LANES), lambda i: (0, 0))),
        scratch_shapes=[pltpu.VMEM((SUBLANES, LANES), F32)],
        compiler_params=_cparams(("arbitrary",)),
        name="moe_plan",
    )(e4, r4, cnt)


def _tile_at(ref, r, lead=()):
    return ref.at[lead + (pl.ds(pl.multiple_of(r * SUBLANES, SUBLANES), SUBLANES), slice(None))]


def _issue(copy_of, count):
    def start(i, carry):
        for u in range(DMA_UNROLL):
            copy_of(i * DMA_UNROLL + u).start(priority=u % 2)
        return carry

    lax.fori_loop(0, count // DMA_UNROLL, start, 0)


def _wait(copy_of, count):
    def wait(pidx, carry):
        copy_of(pidx).wait()
        return carry

    lax.fori_loop(0, count, wait, 0, unroll=DMA_UNROLL)


def _dispatch_body(zs_ref, ze_ref, dest_ref, x_ref, xs_ref, zero_scr, sem, zsem, *, n_experts):
    def token_copy(pidx):
        t = lax.shift_right_logical(pidx, 2)
        return pltpu.make_async_copy(_tile_at(x_ref, t), _tile_at(xs_ref, dest_ref[0, 0, pidx]), sem)

    _issue(token_copy, TOK_TILE * TOP_K)
    _wait(token_copy, TOK_TILE * TOP_K)

    @pl.when(pl.program_id(0) == pl.num_programs(0) - 1)
    def _():
        zero_scr[...] = jnp.zeros_like(zero_scr)

        def zero_rows(first, nrows, wait):
            cp = pltpu.make_async_copy(zero_scr.at[pl.ds(0, nrows * SUBLANES), :],
                                       xs_ref.at[pl.ds(pl.multiple_of(first * SUBLANES, SUBLANES),
                                                       nrows * SUBLANES), :], zsem)
            cp.wait() if wait else cp.start()

        def per_expert(wait):
            def body(e, carry):
                first = zs_ref[e]
                count = ze_ref[e] - first
                nbig = lax.shift_right_logical(count, ZERO_ROWS.bit_length() - 1)

                def big(j, c):
                    zero_rows(first + j * ZERO_ROWS, ZERO_ROWS, wait)
                    return c

                lax.fori_loop(0, nbig, big, 0)
                cur = first + nbig * ZERO_ROWS
                piece = ZERO_ROWS // 2
                while piece >= 1:
                    has = lax.bitwise_and(count, piece)

                    @pl.when(has != 0)
                    def _(cur=cur, piece=piece):
                        zero_rows(cur, piece, wait)

                    cur = cur + has
                    piece //= 2
                return carry
            lax.fori_loop(0, n_experts, body, 0)

        per_expert(False)
        per_expert(True)


def _dispatch(x1t, dest3, zstart, zend, rows, n_experts):
    n = x1t.shape[0] // SUBLANES
    npairs = TOK_TILE * TOP_K
    grid_spec = pltpu.PrefetchScalarGridSpec(
        num_scalar_prefetch=2,
        grid=(n // TOK_TILE,),
        in_specs=[pl.BlockSpec((1, 1, npairs), lambda i, zs, ze: (i, 0, 0), memory_space=pltpu.SMEM),
                  pl.BlockSpec((TOK_TILE * SUBLANES, LANES), lambda i, zs, ze: (i, 0))],
        out_specs=pl.BlockSpec(memory_space=pl.ANY),
        scratch_shapes=[pltpu.VMEM((ZERO_ROWS * SUBLANES, LANES), F32), pltpu.SemaphoreType.DMA(()),
                        pltpu.SemaphoreType.DMA(())],
    )
    return pl.pallas_call(
        functools.partial(_dispatch_body, n_experts=n_experts),
        out_shape=jax.ShapeDtypeStruct((rows * SUBLANES, LANES), F32),
        grid_spec=grid_spec,
        compiler_params=_cparams(("arbitrary",)),
        name="moe_dispatch",
    )(zstart, zend, dest3, x1t)


def _ffn_body(blk_ref, nused_ref, segend_ref, xs_ref, bgu_ref, bd_ref, wgu_hbm, wd_hbm, ys_ref,
              wgu_f32, wd_f32, wgu_bf, wd_bf, sems, slot_ref, *, de, layer):
    i = pl.program_id(0)
    nused = nused_ref[0]
    e = blk_ref[i]
    prev = blk_ref[jnp.maximum(i - 1, 0)]

    def weight_copies(expert, slot):
        return (pltpu.make_async_copy(wgu_hbm.at[layer, expert], wgu_f32.at[slot], sems.at[0, slot]),
                pltpu.make_async_copy(wd_hbm.at[layer, expert], wd_f32.at[slot], sems.at[1, slot]))

    @pl.when((i == 0) & (nused > 0))
    def _():
        slot_ref[0] = 0
        for c in weight_copies(e, 0):
            c.start()

    @pl.when((i < nused) & ((i == 0) | (e != prev)))
    def _():
        slot = slot_ref[0]
        for c in weight_copies(e, slot):
            c.wait()
        nxt = segend_ref[e]

        @pl.when(nxt < nused)
        def _():
            for c in weight_copies(blk_ref[nxt], 1 - slot):
                c.start()

        rb = 128
        for r in range(0, wgu_bf.shape[0], rb):
            wgu_bf[r:r + rb, :] = wgu_f32[slot, r:r + rb, :].astype(BF16)
        for r in range(0, wd_bf.shape[0], rb):
            wd_bf[r:r + rb, :] = wd_f32[slot, r:r + rb, :].astype(BF16)
        slot_ref[0] = 1 - slot

    @pl.when(i < nused)
    def _():
        xb = _tiles_to_rows(xs_ref, MOE_BM).astype(BF16)
        gu = _dot(xb, wgu_bf[...]) + bgu_ref[...]
        gate = jnp.minimum(gu[:, :de], SWIGLU_LIMIT)
        up = jnp.clip(gu[:, de:], -SWIGLU_LIMIT, SWIGLU_LIMIT)
        act = (up + 1.0) * gate * jax.nn.sigmoid(SWIGLU_ALPHA * gate)
        _rows_to_tiles(ys_ref, _dot(act.astype(BF16), wd_bf[...]) + bd_ref[...])

    @pl.when(i >= nused)
    def _():
        ys_ref[...] = jnp.zeros_like(ys_ref)


def _ffn(xs, blk_e, nused, segend_blk, w_gu, b_gu, w_down, b_down, layer):
    rows = xs.shape[0] // SUBLANES
    depth, n_experts, d, de2 = w_gu.shape
    de = de2 // 2
    nb = rows // MOE_BM
    tile_blk = pl.BlockSpec((MOE_BM * SUBLANES, LANES), lambda i, blk, nu, se: (i, 0))
    grid_spec = pltpu.PrefetchScalarGridSpec(
        num_scalar_prefetch=3,
        grid=(nb,),
        in_specs=[tile_blk,
                  pl.BlockSpec((None, None, 1, de2), lambda i, blk, nu, se: (layer, blk[i], 0, 0)),
                  pl.BlockSpec((None, None, 1, d), lambda i, blk, nu, se: (layer, blk[i], 0, 0)),
                  pl.BlockSpec(memory_space=pl.ANY),
                  pl.BlockSpec(memory_space=pl.ANY)],
        out_specs=tile_blk,
        scratch_shapes=[pltpu.VMEM((2, d, de2), F32), pltpu.VMEM((2, de, d), F32),
                        pltpu.VMEM((d, de2), BF16), pltpu.VMEM((de, d), BF16),
                        pltpu.SemaphoreType.DMA((2, 2)), pltpu.SMEM((1,), I32)],
    )
    return pl.pallas_call(
        functools.partial(_ffn_body, de=de, layer=layer),
        out_shape=jax.ShapeDtypeStruct((rows * SUBLANES, LANES), F32),
        grid_spec=grid_spec,
        compiler_params=_cparams(("arbitrary",)),
        name="moe_ffn",
    )(blk_e, nused, segend_blk, xs, b_gu.reshape(depth, n_experts, 1, de2),
      b_down.reshape(depth, n_experts, 1, d), w_gu, w_down)


def _combine_body(dest_ref, dest_next_ref, g4_ref, x1_ref, g_ref, b_ref, ys_ref, *rest, alpha, split_tiles):
    *out_refs, buf, sems = rest
    i = pl.program_id(0)
    slot = lax.rem(i, 2)
    npairs = TOK_TILE * TOP_K

    def gather(dref, s):
        def row_copy(pidx):
            t = lax.shift_right_logical(pidx, 2)
            k = lax.bitwise_and(pidx, TOP_K - 1)
            return pltpu.make_async_copy(_tile_at(ys_ref, dref[0, 0, pidx]), _tile_at(buf, t, (s, k)), sems.at[s])
        return row_copy

    @pl.when(i == 0)
    def _():
        _issue(gather(dest_ref, 0), npairs)

    @pl.when(i + 1 < pl.num_programs(0))
    def _():
        _issue(gather(dest_next_ref, 1 - slot), npairs)

    _wait(gather(dest_ref, slot), npairs)
    g4 = g4_ref[...]
    f = g4[:, 0:1] * _tiles_to_rows(buf, TOK_TILE, (slot, 0))
    for k in range(1, TOP_K):
        f = f + g4[:, k:k + 1] * _tiles_to_rows(buf, TOK_TILE, (slot, k))
    res = _layer_norm(alpha * x1_ref[...] + f, g_ref[...], b_ref[...])
    if split_tiles is None:
        out_refs[0][...] = res
    else:

        @pl.when(i < split_tiles)
        def _():
            out_refs[0][...] = res

        @pl.when(i >= split_tiles)
        def _():
            out_refs[1][...] = res


def _combine(ys, dest3, g4, x1, ln_g, ln_b, alpha, n_prompt=None):
    n, d = x1.shape
    full = lambda a: pl.BlockSpec(a.shape, lambda i: (0,) * a.ndim)
    tile = lambda f: pl.BlockSpec((TOK_TILE, d), f)
    if n_prompt is None:
        split_tiles = None
        out_shape = jax.ShapeDtypeStruct((n, d), F32)
        out_specs = tile(lambda i: (i, 0))
    else:
        assert n - n_prompt == TOK_TILE and n_prompt % TOK_TILE == 0
        split_tiles = n_prompt // TOK_TILE
        out_shape = (jax.ShapeDtypeStruct((n_prompt, d), F32), jax.ShapeDtypeStruct((TOK_TILE, d), F32))
        out_specs = (tile(lambda i: (jnp.minimum(i, split_tiles - 1), 0)), tile(lambda i: (0, 0)))
    nsteps = n // TOK_TILE
    dest_spec = lambda f: pl.BlockSpec((1, 1, TOK_TILE * TOP_K), f, memory_space=pltpu.SMEM)
    return pl.pallas_call(
        functools.partial(_combine_body, alpha=alpha, split_tiles=split_tiles),
        out_shape=out_shape,
        grid=(nsteps,),
        in_specs=[dest_spec(lambda i: (i, 0, 0)),
                  dest_spec(lambda i: (jnp.minimum(i + 1, nsteps - 1), 0, 0)),
                  pl.BlockSpec((TOK_TILE, LANES), lambda i: (i, 0)),
                  tile(lambda i: (i, 0)),
                  full(ln_g), full(ln_b),
                  pl.BlockSpec(memory_space=pl.ANY)],
        out_specs=out_specs,
        scratch_shapes=[pltpu.VMEM((2, TOP_K, TOK_TILE * SUBLANES, LANES), F32), pltpu.SemaphoreType.DMA((2,))],
        compiler_params=_cparams(("arbitrary",)),
        name="moe_combine",
    )(dest3, dest3, g4, x1, ln_g, ln_b, ys)


def kernel(x_prompt, x_sample, state_ret, state_gdn, state_conv, w_in, conv_w, a_log, dt_bias, ret_gn_w, ret_gn_b, gdn_norm_w, w_out, ln1_g, ln1_b, w_router, b_router, w_gu, b_gu, w_down, b_down, ln2_g, ln2_b):
    bsz, seq, d = x_prompt.shape
    dbs, dseq, _ = x_sample.shape
    depth = w_in.shape[0]
    n_experts = w_router.shape[-1]
    assert dseq == 1 and dbs == LANES and seq % CHUNK == 0 and d == 2 * HW == SUBLANES * LANES
    assert 2 ** (INV_BASE_LEVELS + 1) == INV_BASE and CHUNK % INV_BASE == 0
    n_prompt = bsz * seq
    n = n_prompt + dbs
    alpha = (2.0 * depth) ** 0.25
    tm = _pick_tile(n, (384, 256, 128))
    tt = _pick_tile(seq, (512, 256, 128))
    rows = (pl.cdiv(n * TOP_K, MOE_BM) + n_experts) * MOE_BM

    x = jnp.concatenate([x_prompt.reshape(n_prompt, d), x_sample.reshape(dbs, d)], axis=0)
    row2 = lambda a: a.reshape(1, -1).astype(F32)
    pad_lanes = lambda a, fill: jnp.concatenate(
        [a.astype(F32), jnp.full((LANES - a.shape[0],), fill, F32)]).reshape(1, LANES)

    ret_p, gdn_p, conv_p, ret_s, gdn_s, conv_s = [], [], [], [], [], []
    for l in range(depth):
        w_in_bf = jnp.pad(w_in[l], ((0, 0), (0, IN_PAD - w_in.shape[-1]))).astype(BF16)
        p = _inproj(x, w_in_bf, tm)
        gnw, gnb, nw = row2(ret_gn_w[l]), row2(ret_gn_b[l]), row2(gdn_norm_w[l])
        alog_row, dtb_row = pad_lanes(a_log[l], 0.0), pad_lanes(dt_bias[l], 0.0)
        cw = conv_w[l].astype(F32)

        o, sr = _retention_prompt(p, n, bsz, seq, tt, gnw, gnb)
        o, sg, ctail = _gdn_prompt(p, o, bsz, seq, tt, cw, alog_row, dtb_row, nw)
        sconv_t = jnp.transpose(state_conv[l], (1, 0, 2))
        o, sr_s, sg_s, sc_s = _mixer_sample(p, o, n_prompt, dbs, state_ret, state_gdn, l, sconv_t,
                                            float(PAST_LEN), cw, alog_row, dtb_row, gnw, gnb, nw)
        ret_p.append(sr)
        gdn_p.append(sg)
        conv_p.append(jnp.transpose(ctail[:, :, SUBLANES - (CONV_W - 1):, :], (0, 2, 1, 3))
                      .reshape(bsz, CONV_W - 1, 3 * HW))
        ret_s.append(sr_s)
        gdn_s.append(sg_s)
        conv_s.append(jnp.transpose(sc_s, (1, 0, 2)))

        wr_bf = jnp.pad(w_router[l], ((0, 0), (0, LANES - n_experts))).astype(BF16)
        br_row = pad_lanes(b_router[l], NEG)
        x1, x1t, e4, r4, g4, cnt = _post_mixer(o, x, w_out[l].astype(BF16), row2(ln1_g[l]), row2(ln1_b[l]),
                                               wr_bf, br_row, alpha, tm)
        d4, blk, seg = _plan(e4, r4, cnt, _largest_tile(n, PLAN_TILE_CAP), n_experts, rows)
        dest3 = d4[:, :TOP_K].reshape(n // TOK_TILE, 1, TOK_TILE * TOP_K)
        xs = _dispatch(x1t, dest3, seg[0, :n_experts], seg[1, :n_experts], rows, n_experts)
        ys = _ffn(xs, blk[:rows // MOE_BM, 0], seg[2, :1], seg[3, :n_experts], w_gu, b_gu, w_down, b_down, l)
        if l + 1 < depth:
            x = _combine(ys, dest3, g4, x1, row2(ln2_g[l]), row2(ln2_b[l]), alpha)
        else:
            y_prompt, y_sample = _combine(ys, dest3, g4, x1, row2(ln2_g[l]), row2(ln2_b[l]), alpha, n_prompt)

    return (y_prompt.reshape(bsz, seq, d), y_sample.reshape(dbs, dseq, d),
            jnp.stack(ret_p), jnp.stack(gdn_p), jnp.stack(conv_p),
            jnp.stack(ret_s), jnp.stack(gdn_s), jnp.stack(conv_s))
```
